```python
import math, functools
import jax, jax.numpy as jnp
from jax import lax
import numpy as np

D_MODEL = 1024
BATCH = 16
SEQ = 2048
DEPTH = 1
DEC_BATCH = 16
DEC_SEQ = 16
PAST_LEN = 2048

CHUNK = 64
N_META = 16
A_HEADS = 8
A_HD = 64
A_VD = 2 * A_HD
B_HEADS = 8
B_DK = 128
B_DV = 128
N_EXPERTS = 32
TOP_K = 4
D_FF = 1024
SWIGLU_ALPHA = 1.702
SWIGLU_LIMIT = 7.0
Q_BLOCK = 128
REC_BLOCK = 16
MOE_BLOCK = 128
EPS = 1e-6
NEG_INF = -1e30

D_A_QK = A_HEADS * 2 * A_HD
D_A_V = A_HEADS * A_VD
D_B = B_HEADS * B_DK
D_BV = B_HEADS * B_DV
IN_WIDTHS = (D_A_QK, D_A_QK, D_A_V, D_B, D_B, D_BV, D_BV, D_MODEL, D_MODEL)
D_IN = 2 * D_A_QK + D_A_V + 2 * D_B + 2 * D_BV + 2 * D_MODEL

kernel_name = 'hybrid_diffattn_hgrn2_moe_stream_step'


def rms_norm(x, g):
    xf = x.astype(jnp.float32)
    y = xf * lax.rsqrt(jnp.mean(xf * xf, axis=-1, keepdims=True) + EPS)
    return (y * g.astype(jnp.float32)).astype(x.dtype)


def alibi_slopes():
    return jnp.asarray(np.power(2.0, -8.0 * np.arange(1, A_HEADS + 1) / A_HEADS).astype(np.float32))


def chunk_index(pos):
    return jnp.where(pos < N_META, 0, (pos - N_META) // CHUNK + 1)


def diff_attention_core(q, k, v, lam, allowed, dist):
    bias = jnp.where(allowed[None], -alibi_slopes()[:, None, None] * dist[None], NEG_INF)
    s = jnp.einsum('bqhmd,bkhmd->bhmqk', q, k).astype(jnp.float32) * (A_HD ** -0.5) + bias[None, :, None]
    p = jax.nn.softmax(s, axis=-1)
    w = p[:, :, 0] - lam * p[:, :, 1]
    return jnp.einsum('bhqk,bkhv->bqhv', w.astype(v.dtype), v)


def prompt_attention(q, k, v, lam):
    B, L = q.shape[:2]
    n_qb = -(-L // Q_BLOCK)
    Lp = n_qb * Q_BLOCK
    pad = Lp - L
    padt = lambda a: jnp.pad(a, [(0, 0), (0, pad)] + [(0, 0)] * (a.ndim - 2))
    qp, kp, vp = padt(q), padt(k), padt(v)
    pos = jnp.arange(Lp)
    cid = chunk_index(pos)
    key_ok = pos < L
    q_blocks = qp.reshape(B, n_qb, Q_BLOCK, A_HEADS, 2, A_HD).swapaxes(0, 1)
    pos_blocks = pos.reshape(n_qb, Q_BLOCK)

    def one_block(args):
        qb, qpos = args
        allowed = (cid[None, :] <= chunk_index(qpos)[:, None]) & key_ok[None, :]
        dist = jnp.abs(qpos[:, None] - pos[None, :]).astype(jnp.float32)
        return diff_attention_core(qb, kp, vp, lam, allowed, dist)

    o = lax.map(one_block, (q_blocks, pos_blocks))
    return o.swapaxes(0, 1).reshape(B, Lp, A_HEADS, A_VD)[:, :L]


def sample_attention(q, k, v, lam, cache_k, cache_v):
    B, T = q.shape[:2]
    P = cache_k.shape[1]
    keys = jnp.concatenate([cache_k.reshape(B, P, A_HEADS, 2, A_HD).astype(k.dtype), k], axis=1)
    vals = jnp.concatenate([cache_v.astype(v.dtype), v], axis=1)
    qpos = P + jnp.arange(T)
    kpos = jnp.arange(P + T)
    allowed = jnp.ones((T, P + T), bool)
    dist = jnp.abs(qpos[:, None] - kpos[None, :]).astype(jnp.float32)
    return diff_attention_core(q, keys, vals, lam, allowed, dist)


def hgrn2_scan(q, k, v, log_f, s0):
    B, T, H, DK = q.shape
    DV = v.shape[-1]
    nb = -(-T // REC_BLOCK)
    pad = nb * REC_BLOCK - T

    def blocks(a):
        a = jnp.pad(a, ((0, 0), (0, pad), (0, 0), (0, 0)))
        return a.reshape(B, nb, REC_BLOCK, H, a.shape[-1]).swapaxes(0, 1)

    causal = jnp.tril(jnp.ones((REC_BLOCK, REC_BLOCK), jnp.float32))

    def step(S, blk):
        qb, kb, vb, gb = blk
        b = jnp.cumsum(gb, axis=1)
        b_last = b[:, -1]
        q_dec = qb * jnp.exp(b)
        k_inv = kb * jnp.exp(-b)
        o = jnp.einsum('bthk,bhkv->bthv', q_dec, S)
        a = jnp.einsum('bthk,bshk->bhts', q_dec, k_inv) * causal
        o = o + jnp.einsum('bhts,bshv->bthv', a, vb)
        k_end = kb * jnp.exp(b_last[:, None] - b)
        S = jnp.exp(b_last)[..., None] * S + jnp.einsum('bshk,bshv->bhkv', k_end, vb)
        return S, o

    S, o = lax.scan(step, s0.astype(jnp.float32), (blocks(q), blocks(k), blocks(v), blocks(log_f)))
    o = o.swapaxes(0, 1).reshape(B, nb * REC_BLOCK, H, DV)[:, :T]
    return o, S


def moe(h, w_router, b_router, w1, b1, w2, b2):
    B, T, D = h.shape
    N = B * T
    NK = N * TOP_K
    xt = h.reshape(N, D)
    logits = (xt @ w_router + b_router).astype(jnp.float32)
    top_val, top_idx = lax.top_k(logits, TOP_K)
    gates = jax.nn.softmax(top_val, axis=-1)
    eid = top_idx.reshape(-1)
    tid = jnp.repeat(jnp.arange(N, dtype=jnp.int32), TOP_K)
    gw = gates.reshape(-1)
    order = jnp.argsort(eid)
    e_s, t_s, g_s = eid[order], tid[order], gw[order]
    counts = jnp.bincount(eid, length=N_EXPERTS)
    pcounts = (counts + MOE_BLOCK - 1) // MOE_BLOCK * MOE_BLOCK
    pend = jnp.cumsum(pcounts)
    pstart = pend - pcounts
    ustart = jnp.cumsum(counts) - counts
    slot = pstart[e_s] + jnp.arange(NK) - ustart[e_s]
    n_blocks = -(-NK // MOE_BLOCK) + N_EXPERTS
    P = n_blocks * MOE_BLOCK
    tok_buf = jnp.full((P,), N, jnp.int32).at[slot].set(t_s)
    gate_buf = jnp.zeros((P,), jnp.float32).at[slot].set(g_s)
    blk_e = jnp.minimum(jnp.searchsorted(pend, jnp.arange(n_blocks) * MOE_BLOCK, side='right'), N_EXPERTS - 1)
    x_pad = jnp.concatenate([xt, jnp.zeros((1, D), xt.dtype)], axis=0)

    def expert_block(args):
        tok, e = args
        hb = x_pad[tok] @ w1[e] + b1[e]
        gate = jnp.minimum(hb[:, 0::2], SWIGLU_LIMIT)
        up = jnp.clip(hb[:, 1::2], -SWIGLU_LIMIT, SWIGLU_LIMIT)
        act = gate * jax.nn.sigmoid(SWIGLU_ALPHA * gate) * (up + 1.0)
        return act @ w2[e] + b2[e]

    out = lax.map(expert_block, (tok_buf.reshape(n_blocks, MOE_BLOCK), blk_e)).reshape(P, D)
    y = jax.ops.segment_sum(out * gate_buf[:, None], tok_buf, num_segments=N + 1)[:N]
    return y.reshape(B, T, D).astype(h.dtype)


def encoder_layer(x, lw, layer_idx, lb, attend, s0):
    B, T, _ = x.shape
    lam_init = 0.8 - 0.6 * math.exp(-0.3 * layer_idx)
    h = rms_norm(x, lw['norm_mix_g'])
    z = h @ lw['w_in']
    splits = [int(s) for s in np.cumsum(IN_WIDTHS)[:-1]]
    q_a, k_a, v_a, q_b, f_b, i_b, g_b, gate_a, gate_b = jnp.split(z, splits, axis=-1)

    q_a = rms_norm(q_a.reshape(B, T, A_HEADS, 2, A_HD), lw['q_norm_g'])
    k_a = rms_norm(k_a.reshape(B, T, A_HEADS, 2, A_HD), lw['k_norm_g'])
    v_a = v_a.reshape(B, T, A_HEADS, A_VD)
    f32 = jnp.float32
    lam = (jnp.exp(jnp.sum(lw['lambda_q1'].astype(f32) * lw['lambda_k1'].astype(f32)))
           - jnp.exp(jnp.sum(lw['lambda_q2'].astype(f32) * lw['lambda_k2'].astype(f32))) + lam_init)
    o_a = attend(q_a, k_a, v_a, lam)
    o_a = rms_norm(o_a, lw['subln_g']) * (1.0 - lam_init)
    y_a = o_a.reshape(B, T, D_A_V) @ lw['w_a']

    fgate = lb + (1.0 - lb) * jax.nn.sigmoid(f_b.astype(f32))
    heads = lambda a, d: a.reshape(B, T, B_HEADS, d)
    o_b, s_new = hgrn2_scan(heads(jax.nn.silu(q_b.astype(f32)), B_DK), heads(1.0 - fgate, B_DK),
                            heads(i_b.astype(f32), B_DV), heads(jnp.log(fgate), B_DK), s0)
    o_b = rms_norm(o_b, lw['hgrn_norm_g']).astype(x.dtype) * jax.nn.silu(g_b.reshape(B, T, B_HEADS, B_DV))
    y_b = o_b.reshape(B, T, D_BV) @ lw['w_b']

    mix = jax.nn.sigmoid(gate_a) * y_a + jax.nn.sigmoid(gate_b) * y_b
    x = x + (mix @ lw['w_o']).astype(x.dtype)
    x = x + moe(rms_norm(x, lw['norm_ffn_g']), lw['w_router'], lw['b_router'],
                lw['w_mlp1'], lw['b_mlp1'], lw['w_mlp2'], lw['b_mlp2'])
    return x, k_a.reshape(B, T, A_HEADS, 2 * A_HD), v_a, s_new


def setup_inputs(seed: int = 0) -> dict:
    key = jax.random.key(seed)
    ks = jax.random.split(key, 27)
    nrm = lambda k, shape, scale: jax.random.normal(k, shape, jnp.float32) * scale
    gain = lambda k, shape: 1.0 + 0.05 * jax.random.normal(k, shape, jnp.float32)
    return {
        'x_prompt': nrm(ks[0], (BATCH, SEQ, D_MODEL), 1.0),
        'x_sample': nrm(ks[1], (DEC_BATCH, DEC_SEQ, D_MODEL), 1.0),
        'cache_k': nrm(ks[2], (DEPTH, DEC_BATCH, PAST_LEN, A_HEADS, 2 * A_HD), 1.0),
        'cache_v': nrm(ks[3], (DEPTH, DEC_BATCH, PAST_LEN, A_HEADS, A_VD), 1.0),
        'state_hgrn': nrm(ks[4], (DEPTH, DEC_BATCH, B_HEADS, B_DK, B_DV), 0.5),
        'meta_tokens': nrm(ks[5], (N_META, D_MODEL), 1.0),
        'norm_mix_g': gain(ks[6], (DEPTH, D_MODEL)),
        'w_in': nrm(ks[7], (DEPTH, D_MODEL, D_IN), D_MODEL ** -0.5),
        'q_norm_g': gain(ks[8], (DEPTH, A_HD)),
        'k_norm_g': gain(ks[9], (DEPTH, A_HD)),
        'lambda_q1': nrm(ks[10], (DEPTH, A_HD), 0.1),
        'lambda_k1': nrm(ks[11], (DEPTH, A_HD), 0.1),
        'lambda_q2': nrm(ks[12], (DEPTH, A_HD), 0.1),
        'lambda_k2': nrm(ks[13], (DEPTH, A_HD), 0.1),
        'subln_g': gain(ks[14], (DEPTH, A_VD)),
        'hgrn_lb': nrm(ks[15], (DEPTH + 1, D_B), 0.1),
        'hgrn_norm_g': gain(ks[16], (DEPTH, B_DV)),
        'w_a': nrm(ks[17], (DEPTH, D_A_V, D_MODEL), D_A_V ** -0.5),
        'w_b': nrm(ks[18], (DEPTH, D_BV, D_MODEL), D_BV ** -0.5),
        'w_o': nrm(ks[19], (DEPTH, D_MODEL, D_MODEL), D_MODEL ** -0.5),
        'norm_ffn_g': gain(ks[20], (DEPTH, D_MODEL)),
        'w_router': nrm(ks[21], (DEPTH, D_MODEL, N_EXPERTS), D_MODEL ** -0.5),
        'b_router': nrm(ks[22], (DEPTH, N_EXPERTS), 0.01),
        'w_mlp1': nrm(ks[23], (DEPTH, N_EXPERTS, D_MODEL, 2 * D_FF), D_MODEL ** -0.5),
        'b_mlp1': nrm(ks[24], (DEPTH, N_EXPERTS, 2 * D_FF), 0.01),
        'w_mlp2': nrm(ks[25], (DEPTH, N_EXPERTS, D_FF, D_MODEL), D_FF ** -0.5),
        'b_mlp2': nrm(ks[26], (DEPTH, N_EXPERTS, D_MODEL), 0.01),
    }


def reference(x_prompt, x_sample, cache_k, cache_v, state_hgrn, meta_tokens, norm_mix_g, w_in,
              q_norm_g, k_norm_g, lambda_q1, lambda_k1, lambda_q2, lambda_k2, subln_g, hgrn_lb,
              hgrn_norm_g, w_a, w_b, w_o, norm_ffn_g, w_router, b_router, w_mlp1, b_mlp1,
              w_mlp2, b_mlp2):
    B = x_prompt.shape[0]
    meta = jnp.broadcast_to(meta_tokens[None].astype(x_prompt.dtype), (B, N_META, x_prompt.shape[-1]))
    xp = jnp.concatenate([meta, x_prompt], axis=1)
    xs = x_sample
    lower_bounds = jnp.cumsum(jax.nn.softmax(hgrn_lb.astype(jnp.float32), axis=0), axis=0)
    s0_prompt = jnp.zeros((B, B_HEADS, B_DK, B_DV), jnp.float32)
    kp_all, vp_all, sp_all, ks_all, vs_all, ss_all = [], [], [], [], [], []
    for l in range(DEPTH):
        lw = {'norm_mix_g': norm_mix_g[l], 'w_in': w_in[l], 'q_norm_g': q_norm_g[l],
              'k_norm_g': k_norm_g[l], 'lambda_q1': lambda_q1[l], 'lambda_k1': lambda_k1[l],
              'lambda_q2': lambda_q2[l], 'lambda_k2': lambda_k2[l], 'subln_g': subln_g[l],
              'hgrn_norm_g': hgrn_norm_g[l], 'w_a': w_a[l], 'w_b': w_b[l], 'w_o': w_o[l],
              'norm_ffn_g': norm_ffn_g[l], 'w_router': w_router[l], 'b_router': b_router[l],
              'w_mlp1': w_mlp1[l], 'b_mlp1': b_mlp1[l], 'w_mlp2': w_mlp2[l], 'b_mlp2': b_mlp2[l]}
        xp, kp, vp, sp = encoder_layer(xp, lw, l, lower_bounds[l], prompt_attention, s0_prompt)
        attend_s = functools.partial(sample_attention, cache_k=cache_k[l], cache_v=cache_v[l])
        xs, k_s, v_s, s_s = encoder_layer(xs, lw, l, lower_bounds[l], attend_s, state_hgrn[l])
        kp_all.append(kp)
        vp_all.append(vp)
        sp_all.append(sp)
        ks_all.append(k_s)
        vs_all.append(v_s)
        ss_all.append(s_s)
    y_prompt = xp[:, N_META:]
    return (y_prompt, xs, jnp.stack(kp_all), jnp.stack(vp_all), jnp.stack(sp_all),
            jnp.stack(ks_all), jnp.stack(vs_all), jnp.stack(ss_all))
```

```python
import functools
import math

import numpy as np
import jax
import jax.numpy as jnp
from jax import lax
from jax.experimental import pallas as pl
from jax.experimental.pallas import tpu as pltpu

F32 = jnp.float32
BF16 = jnp.bfloat16

CHUNK = 64
N_META = 16
A_HD = 64
TOP_K = 4
SWIGLU_ALPHA = 1.702
SWIGLU_LIMIT = 7.0
EPS = 1e-6
NEG_INF = -1e30
REC_SUB = 16

LANES = 128
ATT_BLOCK = 2 * CHUNK
VMEM_LIMIT = 48 * 1024 * 1024


def _pick_block(n, candidates):
    for c in candidates:
        if n % c == 0:
            return c
    raise ValueError(f"no block size in {candidates} divides {n}")


def _dot(a, b, **kw):
    return jnp.dot(a, b, preferred_element_type=F32, **kw)


def _dot_nt(a, b):
    return lax.dot_general(a, b, (((1,), (1,)), ((), ())), preferred_element_type=F32)


def _dot_tn(a, b):
    return lax.dot_general(a, b, (((0,), (0,)), ((), ())), preferred_element_type=F32)


def _params(*sem):
    return pltpu.CompilerParams(dimension_semantics=sem, vmem_limit_bytes=VMEM_LIMIT)


def _rmsnorm_kernel(x_ref, g_ref, o_ref):
    x = x_ref[...]
    ms = jnp.mean(x * x, axis=-1, keepdims=True)
    o_ref[...] = (x * lax.rsqrt(ms + EPS) * g_ref[...]).astype(o_ref.dtype)


def rmsnorm_rows(x, g, tm):
    n, d = x.shape
    return pl.pallas_call(
        _rmsnorm_kernel,
        out_shape=jax.ShapeDtypeStruct((n, d), BF16),
        grid=(n // tm,),
        in_specs=[pl.BlockSpec((tm, d), lambda i: (i, 0)),
                  pl.BlockSpec((1, d), lambda i: (0, 0))],
        out_specs=pl.BlockSpec((tm, d), lambda i: (i, 0)),
        compiler_params=_params("parallel"),
        name="rmsnorm_rows",
    )(x, g.reshape(1, d))


def _proj_kernel(h_ref, w_ref, o_ref):
    o_ref[...] = _dot(h_ref[...], w_ref[...]).astype(o_ref.dtype)


def _proj_groupnorm_kernel(h_ref, w_ref, gsum_ref, gexp_ref, gain_ref, o_ref):
    z = _dot(h_ref[...], w_ref[...])
    ms = _dot((z * z).astype(BF16), gsum_ref[...])
    inv = lax.rsqrt(ms + EPS)
    inv_hi = inv.astype(BF16)
    inv_lo = (inv - inv_hi.astype(F32)).astype(BF16)
    inv_full = _dot(inv_hi, gexp_ref[...]) + _dot(inv_lo, gexp_ref[...])
    o_ref[...] = (z * inv_full * gain_ref[...]).astype(o_ref.dtype)


def project(h, w, out_dtype, tm, tn, group_gain=None, group=A_HD):
    n, k = h.shape
    m = w.shape[1]
    grid = (m // tn, n // tm)
    h_spec = pl.BlockSpec((tm, k), lambda j, i: (i, 0))
    w_spec = pl.BlockSpec((k, tn), lambda j, i: (0, j))
    o_spec = pl.BlockSpec((tm, tn), lambda j, i: (i, j))
    out_shape = jax.ShapeDtypeStruct((n, m), out_dtype)
    if group_gain is None:
        return pl.pallas_call(
            _proj_kernel, out_shape=out_shape, grid=grid,
            in_specs=[h_spec, w_spec], out_specs=o_spec,
            compiler_params=_params("parallel", "parallel"), name="project",
        )(h, w)
    ng = tn // group
    cols = np.arange(tn) // group
    gsum = jnp.asarray((cols[:, None] == np.arange(ng)[None, :]) / group, BF16)
    gexp = jnp.asarray(np.arange(ng)[:, None] == cols[None, :], BF16)
    return pl.pallas_call(
        _proj_groupnorm_kernel, out_shape=out_shape, grid=grid,
        in_specs=[h_spec, w_spec,
                  pl.BlockSpec((tn, ng), lambda j, i: (0, 0)),
                  pl.BlockSpec((ng, tn), lambda j, i: (0, 0)),
                  pl.BlockSpec((1, tn), lambda j, i: (0, j))],
        out_specs=o_spec,
        compiler_params=_params("parallel", "parallel"), name="project_groupnorm",
    )(h, w, gsum, gexp, group_gain.reshape(1, m).astype(F32))


def _map_masks(q):
    lane = lax.broadcasted_iota(jnp.int32, q.shape, 1)
    zero = jnp.zeros_like(q)
    return (jnp.where(lane < A_HD, q, zero), jnp.where(lane >= A_HD, q, zero))


def _subln(o, g_ref, post_scale):
    ms = jnp.mean(o * o, axis=-1, keepdims=True)
    return o * lax.rsqrt(ms + EPS) * g_ref[...] * post_scale


def _attn_prompt_kernel(lam_ref, slope_ref, q_ref, k_ref, v_ref, g_ref, o_ref, kb, vb,
                        *, seq_len, post_scale):
    tb = ATT_BLOCK
    slope = slope_ref[pl.program_id(1)]
    lam = lam_ref[0]
    kb[...] = k_ref[...].astype(BF16)
    vb[...] = v_ref[...].astype(BF16)
    n_blocks = (seq_len - N_META) // tb

    qm = _map_masks(q_ref[0:N_META, :])
    r = lax.broadcasted_iota(jnp.int32, (N_META, N_META), 0)
    c = lax.broadcasted_iota(jnp.int32, (N_META, N_META), 1)
    bias = -slope * jnp.abs(r - c).astype(F32)
    km = kb[0:N_META, :]
    vm = vb[0:N_META, :]
    ws = []
    for m in range(2):
        s = _dot_nt(qm[m], km) + bias
        p = jnp.exp(s - jnp.max(s, axis=-1, keepdims=True))
        ws.append(p / jnp.sum(p, axis=-1, keepdims=True))
    w = ws[0] - lam * ws[1]
    o_ref[0:N_META, :] = _subln(_dot(w.astype(BF16), vm), g_ref, post_scale).astype(o_ref.dtype)

    r = lax.broadcasted_iota(jnp.int32, (tb, tb), 0)
    c = lax.broadcasted_iota(jnp.int32, (tb, tb), 1)
    rel_bias = -slope * (r - c).astype(F32)
    allowed = (c // CHUNK) <= (r // CHUNK)
    diag_bias = jnp.where(allowed, -slope * jnp.abs(r - c).astype(F32), NEG_INF)
    rm = lax.broadcasted_iota(jnp.int32, (tb, N_META), 0)
    cm = lax.broadcasted_iota(jnp.int32, (tb, N_META), 1)
    meta_rel = (rm - cm).astype(F32)

    def online(state, s, vblk, shift):
        m_prev, l_prev, acc = state
        m_new = jnp.maximum(m_prev, jnp.max(s, axis=-1, keepdims=True) + shift)
        alpha = jnp.exp(m_prev - m_new)
        p = jnp.exp(s - (m_new - shift))
        l_new = alpha * l_prev + jnp.sum(p, axis=-1, keepdims=True)
        acc_new = alpha * acc + _dot(p.astype(BF16), vblk)
        return m_new, l_new, acc_new

    for i in range(n_blocks):
        q0 = N_META + i * tb
        qm = _map_masks(q_ref[q0:q0 + tb, :])
        states = []
        for m in range(2):
            s = _dot_nt(qm[m], km) - slope * (meta_rel + float(q0))
            mx = jnp.max(s, axis=-1, keepdims=True)
            p = jnp.exp(s - mx)
            states.append((mx, jnp.sum(p, axis=-1, keepdims=True), _dot(p.astype(BF16), vm)))

        if i > 0:
            def past(j, carry, i=i, qm=qm):
                k0 = pl.multiple_of(N_META + j * tb, N_META)
                kblk = kb[pl.ds(k0, tb), :]
                vblk = vb[pl.ds(k0, tb), :]
                shift = -slope * jnp.asarray((i - j) * tb, F32)
                st = (carry[0:3], carry[3:6])
                out = []
                for m in range(2):
                    out.extend(online(st[m], _dot_nt(qm[m], kblk) + rel_bias, vblk, shift))
                return tuple(out)

            carry = lax.fori_loop(0, i, past, tuple(states[0]) + tuple(states[1]))
            states = [carry[0:3], carry[3:6]]

        kblk = kb[q0:q0 + tb, :]
        vblk = vb[q0:q0 + tb, :]
        outs = []
        for m in range(2):
            _, l_fin, acc = online(states[m], _dot_nt(qm[m], kblk) + diag_bias, vblk, 0.0)
            outs.append(acc / l_fin)
        o = outs[0] - lam * outs[1]
        o_ref[q0:q0 + tb, :] = _subln(o, g_ref, post_scale).astype(o_ref.dtype)


def attention_prompt(q, k, v, lam, slopes, subln_g, batch, seq_len, heads, row_block0, post_scale):
    hw = 2 * A_HD
    spec = lambda: pl.BlockSpec((seq_len, hw), lambda b, h, *_: (row_block0 + b, h))
    kern = functools.partial(_attn_prompt_kernel, seq_len=seq_len, post_scale=post_scale)
    return pl.pallas_call(
        kern,
        out_shape=jax.ShapeDtypeStruct((batch * seq_len, heads * hw), BF16),
        grid_spec=pltpu.PrefetchScalarGridSpec(
            num_scalar_prefetch=2, grid=(batch, heads),
            in_specs=[spec(), spec(), spec(), pl.BlockSpec((1, hw), lambda b, h, *_: (0, 0))],
            out_specs=pl.BlockSpec((seq_len, hw), lambda b, h, *_: (b, h)),
            scratch_shapes=[pltpu.VMEM((seq_len, hw), BF16), pltpu.VMEM((seq_len, hw), BF16)]),
        compiler_params=_params("parallel", "parallel"), name="attention_prompt",
    )(lam, slopes, q, k, v, subln_g.reshape(1, hw))


def _attn_sample_kernel(lam_ref, slope_ref, q_ref, k_ref, v_ref, ck_ref, cv_ref, g_ref, o_ref,
                        *, post_scale):
    slope = slope_ref[pl.program_id(1)]
    lam = lam_ref[0]
    t = q_ref.shape[0]
    past = ck_ref.shape[1]
    qm = _map_masks(q_ref[...])
    kn = k_ref[...].astype(BF16)
    vn = v_ref[...].astype(BF16)
    kc = ck_ref[0].astype(BF16)
    vc = cv_ref[0].astype(BF16)
    r = lax.broadcasted_iota(jnp.int32, (t, past), 0)
    c = lax.broadcasted_iota(jnp.int32, (t, past), 1)
    bias_c = -slope * (past + r - c).astype(F32)
    r = lax.broadcasted_iota(jnp.int32, (t, t), 0)
    c = lax.broadcasted_iota(jnp.int32, (t, t), 1)
    bias_n = -slope * jnp.abs(r - c).astype(F32)
    wc, wn = [], []
    for m in range(2):
        sc = _dot_nt(qm[m], kc) + bias_c
        sn = _dot_nt(qm[m], kn) + bias_n
        mx = jnp.maximum(jnp.max(sc, axis=-1, keepdims=True), jnp.max(sn, axis=-1, keepdims=True))
        pc = jnp.exp(sc - mx)
        pn = jnp.exp(sn - mx)
        den = jnp.sum(pc, axis=-1, keepdims=True) + jnp.sum(pn, axis=-1, keepdims=True)
        wc.append(pc / den)
        wn.append(pn / den)
    o = (_dot((wc[0] - lam * wc[1]).astype(BF16), vc)
         + _dot((wn[0] - lam * wn[1]).astype(BF16), vn))
    o_ref[...] = _subln(o, g_ref, post_scale).astype(o_ref.dtype)


def attention_sample(q, k, v, cache_k, cache_v, lam, slopes, subln_g, batch, t, heads, row_block0,
                     post_scale):
    hw = 2 * A_HD
    past = cache_k.shape[1]
    new = lambda: pl.BlockSpec((t, hw), lambda b, h, *_: (row_block0 + b, h))
    cache = lambda: pl.BlockSpec((1, past, hw), lambda b, h, *_: (b, 0, h))
    kern = functools.partial(_attn_sample_kernel, post_scale=post_scale)
    return pl.pallas_call(
        kern,
        out_shape=jax.ShapeDtypeStruct((batch * t, heads * hw), BF16),
        grid_spec=pltpu.PrefetchScalarGridSpec(
            num_scalar_prefetch=2, grid=(batch, heads),
            in_specs=[new(), new(), new(), cache(), cache(),
                      pl.BlockSpec((1, hw), lambda b, h, *_: (0, 0))],
            out_specs=pl.BlockSpec((t, hw), lambda b, h, *_: (b, h))),
        compiler_params=_params("parallel", "parallel"), name="attention_sample",
    )(lam, slopes, q, k, v, cache_k, cache_v, subln_g.reshape(1, hw))


def _hgrn_chunk(start, size, heads_per_step, refs, st_ref, consts):
    q_ref, f_ref, i_ref, g_ref, lb_ref, ng_ref, o_ref = refs
    tri, sub_start = consts
    nsub = size // REC_SUB
    rows = pl.ds(start, size)
    for p in range(heads_per_step):
        lanes = slice(p * LANES, (p + 1) * LANES)
        lb = lb_ref[:, lanes]
        fg = lb + (1.0 - lb) * jax.nn.sigmoid(f_ref[rows, lanes])
        kk = 1.0 - fg
        logf = jnp.log(fg)
        qq = jax.nn.silu(q_ref[rows, lanes].astype(F32))
        vv = i_ref[rows, lanes].astype(BF16)
        b = _dot(tri, logf, precision=lax.Precision.HIGHEST)
        r_sub = _dot(sub_start, logf, precision=lax.Precision.HIGHEST)
        q_sub = qq * jnp.exp(b - r_sub)
        q_dec = (q_sub * jnp.exp(r_sub)).astype(BF16)
        st = st_ref[p]
        o_inter = _dot_nt(q_dec, st.astype(BF16))
        b_last = b[size - 1:size, :]
        k_end = (kk * jnp.exp(b_last - b)).astype(BF16)
        o_rows = []
        for i in range(nsub):
            hi = (i + 1) * REC_SUB
            lo = i * REC_SUB
            k_rel = (kk[0:hi] * jnp.exp(r_sub[lo:lo + 1, :] - b[0:hi])).astype(BF16)
            a = _dot_nt(q_sub[lo:hi].astype(BF16), k_rel)
            rr = lax.broadcasted_iota(jnp.int32, (REC_SUB, hi), 0)
            cc = lax.broadcasted_iota(jnp.int32, (REC_SUB, hi), 1)
            a = jnp.where(cc <= rr + lo, a, 0.0)
            o_rows.append(o_inter[lo:hi] + _dot(a.astype(BF16), vv[0:hi]))
        o = o_rows[0] if nsub == 1 else jnp.concatenate(o_rows, axis=0)
        ms = jnp.mean(o * o, axis=-1, keepdims=True)
        on = o * lax.rsqrt(ms + EPS) * ng_ref[...]
        gg = g_ref[rows, lanes].astype(F32)
        o_ref[rows, lanes] = (on * jax.nn.silu(gg)).astype(o_ref.dtype)
        st_ref[p] = st * jnp.exp(b_last) + _dot_tn(vv, k_end)


def _hgrn_consts(size):
    r = lax.broadcasted_iota(jnp.int32, (size, size), 0)
    c = lax.broadcasted_iota(jnp.int32, (size, size), 1)
    tri = (c <= r).astype(F32)
    sub_start = (c < (r // REC_SUB) * REC_SUB).astype(F32)
    return tri, sub_start


def _hgrn_kernel(*args, seq_len, heads_per_step, has_state):
    if has_state:
        q_ref, f_ref, i_ref, g_ref, lb_ref, ng_ref, s0_ref, o_ref, s_ref, st_ref = args
    else:
        q_ref, f_ref, i_ref, g_ref, lb_ref, ng_ref, o_ref, s_ref, st_ref = args
    refs = (q_ref, f_ref, i_ref, g_ref, lb_ref, ng_ref, o_ref)
    for p in range(heads_per_step):
        if has_state:
            st_ref[p] = s0_ref[0, p].T
        else:
            st_ref[p] = jnp.zeros(st_ref.shape[1:], F32)
    _hgrn_chunk(0, N_META, heads_per_step, refs, st_ref, _hgrn_consts(N_META))
    n_chunks = (seq_len - N_META) // CHUNK
    if n_chunks:
        consts = _hgrn_consts(CHUNK)

        def body(ci, _):
            start = pl.multiple_of(N_META + ci * CHUNK, N_META)
            _hgrn_chunk(start, CHUNK, heads_per_step, refs, st_ref, consts)
            return 0

        lax.fori_loop(0, n_chunks, body, 0)
    for p in range(heads_per_step):
        s_ref[0, p] = st_ref[p].T


def hgrn2(zr, f_b, lb, norm_g, s0, batch, seq_len, heads, row_block0, heads_per_step):
    dk = LANES
    hpw = heads_per_step * dk
    d = heads * dk
    nhp = heads // heads_per_step
    col = lambda seg: pl.BlockSpec((seq_len, hpw), lambda b, hp, seg=seg: (row_block0 + b, seg * nhp + hp))
    in_specs = [col(0), pl.BlockSpec((seq_len, hpw), lambda b, hp: (row_block0 + b, hp)), col(1), col(2),
                pl.BlockSpec((1, hpw), lambda b, hp: (0, hp)),
                pl.BlockSpec((1, dk), lambda b, hp: (0, 0))]
    args = [zr, f_b, zr, zr, lb.reshape(1, d), norm_g.reshape(1, dk)]
    state_spec = pl.BlockSpec((1, heads_per_step, dk, dk), lambda b, hp: (b, hp, 0, 0))
    if s0 is not None:
        in_specs.append(state_spec)
        args.append(s0)
    kern = functools.partial(_hgrn_kernel, seq_len=seq_len, heads_per_step=heads_per_step,
                             has_state=s0 is not None)
    return pl.pallas_call(
        kern,
        out_shape=(jax.ShapeDtypeStruct((batch * seq_len, d), BF16),
                   jax.ShapeDtypeStruct((batch, heads, dk, dk), F32)),
        grid=(batch, nhp),
        in_specs=in_specs,
        out_specs=(pl.BlockSpec((seq_len, hpw), lambda b, hp: (b, hp)), state_spec),
        scratch_shapes=[pltpu.VMEM((heads_per_step, dk, dk), F32)],
        compiler_params=_params("parallel", "parallel"), name="hgrn2",
    )(*args)


def _merge_kernel(oa_ref, ob_ref, ga_ref, gb_ref, x_ref, wa_ref, wb_ref, wo_ref, g2_ref, wr_ref, br_ref,
                  x1_ref, h2_ref, lg_ref):
    ya = _dot(oa_ref[...], wa_ref[...])
    yb = _dot(ob_ref[...], wb_ref[...])
    mix = jax.nn.sigmoid(ga_ref[...].astype(F32)) * ya + jax.nn.sigmoid(gb_ref[...].astype(F32)) * yb
    x1 = x_ref[...] + _dot(mix.astype(BF16), wo_ref[...])
    x1_ref[...] = x1
    ms = jnp.mean(x1 * x1, axis=-1, keepdims=True)
    h2 = x1 * lax.rsqrt(ms + EPS) * g2_ref[...]
    h2_ref[...] = h2
    lg_ref[...] = _dot(h2, wr_ref[...], precision=lax.Precision.HIGHEST) + br_ref[...]


def merge_branches(oa, ob, zr, x, wa, wb, wo, g2, w_router, b_router, tm):
    n, d = x.shape
    ne = w_router.shape[1]
    row = lambda: pl.BlockSpec((tm, d), lambda i: (i, 0))
    full = lambda a: pl.BlockSpec(a.shape, lambda i: (0, 0))
    g2 = g2.reshape(1, d)
    b_router = b_router.reshape(1, ne)
    return pl.pallas_call(
        _merge_kernel,
        out_shape=(jax.ShapeDtypeStruct((n, d), F32), jax.ShapeDtypeStruct((n, d), F32),
                   jax.ShapeDtypeStruct((n, ne), F32)),
        grid=(n // tm,),
        in_specs=[row(), row(),
                  pl.BlockSpec((tm, d), lambda i: (i, 3)), pl.BlockSpec((tm, d), lambda i: (i, 4)),
                  row(), full(wa), full(wb), full(wo), full(g2), full(w_router), full(b_router)],
        out_specs=(row(), row(), pl.BlockSpec((tm, ne), lambda i: (i, 0))),
        compiler_params=_params("parallel"), name="merge_branches",
    )(oa, ob, zr, zr, x, wa, wb, wo, g2, w_router, b_router)


def _row_copy(src_hbm, row, dst, r, sem):
    return pltpu.make_async_copy(src_hbm.at[pl.ds(row, 1)], dst.at[pl.ds(r, 1)], sem)


def _moe_ffn_kernel(blk_e_ref, tok_ref, tok_next_ref, h_hbm, w1g_ref, w1u_ref, b1g_ref, b1u_ref,
                    w2_ref, b2_ref, o_ref, xbuf, sem):
    i = pl.program_id(0)
    nb = pl.num_programs(0)
    tm = xbuf.shape[1]
    slot = lax.rem(i, 2)

    def start_rows(ids_ref, s):
        def body(r, _):
            _row_copy(h_hbm, ids_ref[0, 0, r], xbuf.at[s], r, sem.at[s]).start()
            return 0
        lax.fori_loop(0, tm, body, 0, unroll=8)

    @pl.when(i == 0)
    def _():
        start_rows(tok_ref, 0)

    @pl.when(i + 1 < nb)
    def _():
        start_rows(tok_next_ref, 1 - slot)

    def wait_body(r, _):
        _row_copy(h_hbm, 0, xbuf.at[slot], r, sem.at[slot]).wait()
        return 0
    lax.fori_loop(0, tm, wait_body, 0, unroll=8)

    x = xbuf[slot].astype(BF16)
    gate = jnp.minimum(_dot(x, w1g_ref[0]) + b1g_ref[0], SWIGLU_LIMIT)
    up = jnp.clip(_dot(x, w1u_ref[0]) + b1u_ref[0], -SWIGLU_LIMIT, SWIGLU_LIMIT)
    act = gate * jax.nn.sigmoid(SWIGLU_ALPHA * gate) * (up + 1.0)
    o_ref[...] = _dot(act.astype(BF16), w2_ref[0]) + b2_ref[0]


def moe_ffn(h2, tok_blocks, blk_e, w1g, w1u, b1g, b1u, w2, b2, tm):
    n_blocks = tok_blocks.shape[0]
    d = h2.shape[1]
    dff = w2.shape[1]
    wspec = lambda a: pl.BlockSpec((1,) + a.shape[1:], lambda i, e: (e[i], 0, 0))
    ids = lambda f: pl.BlockSpec((1, 1, tm), f, memory_space=pltpu.SMEM)
    return pl.pallas_call(
        _moe_ffn_kernel,
        out_shape=jax.ShapeDtypeStruct((n_blocks * tm, d), F32),
        grid_spec=pltpu.PrefetchScalarGridSpec(
            num_scalar_prefetch=1, grid=(n_blocks,),
            in_specs=[ids(lambda i, e: (i, 0, 0)),
                      ids(lambda i, e: (jnp.minimum(i + 1, n_blocks - 1), 0, 0)),
                      pl.BlockSpec(memory_space=pl.ANY),
                      wspec(w1g), wspec(w1u), wspec(b1g), wspec(b1u), wspec(w2), wspec(b2)],
            out_specs=pl.BlockSpec((tm, d), lambda i, e: (i, 0)),
            scratch_shapes=[pltpu.VMEM((2, tm, d), F32), pltpu.SemaphoreType.DMA((2,))]),
        compiler_params=_params("arbitrary"), name="moe_ffn",
    )(blk_e, tok_blocks, tok_blocks, h2, w1g, w1u, b1g, b1u, w2, b2)


def _moe_combine_kernel(slot_ref, slot_next_ref, gate_ref, x_ref, rows_hbm, o_ref, buf, sem):
    i = pl.program_id(0)
    nb = pl.num_programs(0)
    tm = buf.shape[2]
    slot = lax.rem(i, 2)

    def start_rows(ids_ref, s):
        def body(r, _):
            for k in range(TOP_K):
                _row_copy(rows_hbm, ids_ref[0, 0, k * tm + r], buf.at[s, k], r, sem.at[s]).start()
            return 0
        lax.fori_loop(0, tm, body, 0, unroll=4)

    @pl.when(i == 0)
    def _():
        start_rows(slot_ref, 0)

    @pl.when(i + 1 < nb)
    def _():
        start_rows(slot_next_ref, 1 - slot)

    def wait_body(r, _):
        for k in range(TOP_K):
            _row_copy(rows_hbm, 0, buf.at[slot, k], r, sem.at[slot]).wait()
        return 0
    lax.fori_loop(0, tm, wait_body, 0, unroll=4)

    g = gate_ref[...]
    y = buf[slot, 0] * g[:, 0:1]
    for k in range(1, TOP_K):
        y = y + buf[slot, k] * g[:, k:k + 1]
    o_ref[...] = x_ref[...] + y


def moe_combine(rows, slots, gates, x1, tm):
    n, d = x1.shape
    nb = n // tm
    ids_arr = slots.reshape(nb, tm, TOP_K).transpose(0, 2, 1).reshape(nb, 1, TOP_K * tm)
    ids = lambda f: pl.BlockSpec((1, 1, TOP_K * tm), f, memory_space=pltpu.SMEM)
    return pl.pallas_call(
        _moe_combine_kernel,
        out_shape=jax.ShapeDtypeStruct((n, d), F32),
        grid=(nb,),
        in_specs=[ids(lambda i: (i, 0, 0)),
                  ids(lambda i: (jnp.minimum(i + 1, nb - 1), 0, 0)),
                  pl.BlockSpec((tm, TOP_K), lambda i: (i, 0)),
                  pl.BlockSpec((tm, d), lambda i: (i, 0)),
                  pl.BlockSpec(memory_space=pl.ANY)],
        out_specs=pl.BlockSpec((tm, d), lambda i: (i, 0)),
        scratch_shapes=[pltpu.VMEM((2, TOP_K, tm, d), F32), pltpu.SemaphoreType.DMA((2,))],
        compiler_params=_params("arbitrary"), name="moe_combine",
    )(ids_arr, ids_arr, gates, x1, rows)


def _route(logits, n_experts, tm):
    n = logits.shape[0]
    nk = n * TOP_K
    top_val, top_idx = lax.top_k(logits, TOP_K)
    gates = jax.nn.softmax(top_val, axis=-1)
    eid = top_idx.reshape(-1).astype(jnp.int32)
    tid = jnp.repeat(jnp.arange(n, dtype=jnp.int32), TOP_K)
    order = jnp.argsort(eid)
    e_s = eid[order]
    counts = jnp.bincount(eid, length=n_experts).astype(jnp.int32)
    pcounts = (counts + tm - 1) // tm * tm
    pend = jnp.cumsum(pcounts)
    pstart = pend - pcounts
    ustart = jnp.cumsum(counts) - counts
    slot_sorted = pstart[e_s] + jnp.arange(nk, dtype=jnp.int32) - ustart[e_s]
    n_blocks = -(-nk // tm) + n_experts
    tok_buf = jnp.zeros((n_blocks * tm,), jnp.int32).at[slot_sorted].set(tid[order])
    slots = jnp.zeros((nk,), jnp.int32).at[order].set(slot_sorted).reshape(n, TOP_K)
    blk_e = jnp.minimum(jnp.searchsorted(pend, jnp.arange(n_blocks, dtype=jnp.int32) * tm, side='right'),
                        n_experts - 1).astype(jnp.int32)
    return gates, slots, tok_buf.reshape(n_blocks, 1, tm), blk_e


def kernel(x_prompt, x_sample, cache_k, cache_v, state_hgrn, meta_tokens, norm_mix_g, w_in, q_norm_g,
           k_norm_g, lambda_q1, lambda_k1, lambda_q2, lambda_k2, subln_g, hgrn_lb, hgrn_norm_g, w_a, w_b,
           w_o, norm_ffn_g, w_router, b_router, w_mlp1, b_mlp1, w_mlp2, b_mlp2):
    batch, seq, d = x_prompt.shape
    dec_batch, dec_seq, _ = x_sample.shape
    depth = w_in.shape[0]
    heads = cache_k.shape[3]
    b_heads = state_hgrn.shape[2]
    n_experts = w_router.shape[-1]
    d_ff = w_mlp2.shape[2]
    lp = N_META + seq
    n_p = batch * lp
    n_s = dec_batch * dec_seq
    n = n_p + n_s
    assert dec_seq == N_META and seq % ATT_BLOCK == 0 and n_p % dec_seq == 0
    assert d == heads * 2 * A_HD == b_heads * LANES

    meta = jnp.broadcast_to(meta_tokens[None].astype(x_prompt.dtype), (batch, N_META, d))
    x = jnp.concatenate([jnp.concatenate([meta, x_prompt], axis=1).reshape(n_p, d),
                         x_sample.reshape(n_s, d)], axis=0)

    tm = _pick_block(n, (512, 256, 128, 64, 32, 16))
    tm_moe = 256 if n * TOP_K >= 256 * n_experts else 64
    tm_comb = _pick_block(n, (128, 64, 32, 16))
    slopes = jnp.asarray(np.power(2.0, -8.0 * np.arange(1, heads + 1) / heads).astype(np.float32))
    lower_bounds = jnp.cumsum(jax.nn.softmax(hgrn_lb.astype(F32), axis=0), axis=0)

    k_p, v_p, s_p, k_s, v_s, s_s = [], [], [], [], [], []
    for l in range(depth):
        lam_init = 0.8 - 0.6 * math.exp(-0.3 * l)
        lam = (jnp.exp(jnp.sum(lambda_q1[l].astype(F32) * lambda_k1[l].astype(F32)))
               - jnp.exp(jnp.sum(lambda_q2[l].astype(F32) * lambda_k2[l].astype(F32))) + lam_init).reshape(1)
        w = w_in[l].astype(BF16)
        seg = lambda j: w[:, j * d:(j + 1) * d]

        h = rmsnorm_rows(x, norm_mix_g[l], tm)
        q_a = project(h, seg(0), BF16, tm, d,
                      group_gain=jnp.tile(q_norm_g[l].astype(F32), d // A_HD) * (A_HD ** -0.5))
        k_a = project(h, seg(1), F32, tm, d, group_gain=jnp.tile(k_norm_g[l].astype(F32), d // A_HD))
        v_a = project(h, seg(2), F32, tm, d)
        f_b = project(h, seg(4), F32, tm, d)
        zr = project(h, jnp.concatenate([seg(3), seg(5), seg(6), seg(7), seg(8)], axis=1), BF16, tm, d)

        post = 1.0 - lam_init
        oa_p = attention_prompt(q_a, k_a, v_a, lam, slopes, subln_g[l], batch, lp, heads, 0, post)
        oa_s = attention_sample(q_a, k_a, v_a, cache_k[l].reshape(dec_batch, -1, d),
                                cache_v[l].reshape(dec_batch, -1, d), lam, slopes, subln_g[l],
                                dec_batch, dec_seq, heads, n_p // dec_seq, post)
        oa = jnp.concatenate([oa_p, oa_s], axis=0)

        ob_p, sp = hgrn2(zr, f_b, lower_bounds[l], hgrn_norm_g[l], None, batch, lp, b_heads, 0, 2)
        ob_s, ss = hgrn2(zr, f_b, lower_bounds[l], hgrn_norm_g[l], state_hgrn[l], dec_batch, dec_seq,
                         b_heads, n_p // dec_seq, 2)
        ob = jnp.concatenate([ob_p, ob_s], axis=0)

        x1, h2, logits = merge_branches(oa, ob, zr, x, w_a[l].astype(BF16), w_b[l].astype(BF16),
                                        w_o[l].astype(BF16), norm_ffn_g[l].astype(F32),
                                        w_router[l].astype(F32), b_router[l].astype(F32), tm)

        gates, slots, tok_blocks, blk_e = _route(logits, n_experts, tm_moe)
        w1 = w_mlp1[l]
        rows = moe_ffn(h2, tok_blocks, blk_e,
                       w1[:, :, 0::2].astype(BF16), w1[:, :, 1::2].astype(BF16),
                       b_mlp1[l][:, None, 0::2].astype(F32), b_mlp1[l][:, None, 1::2].astype(F32),
                       w_mlp2[l].astype(BF16), b_mlp2[l][:, None, :].astype(F32), tm_moe)
        x = moe_combine(rows, slots, gates, x1, tm_comb)

        hd = 2 * A_HD
        k_p.append(k_a[:n_p].reshape(batch, lp, heads, hd))
        v_p.append(v_a[:n_p].reshape(batch, lp, heads, hd))
        k_s.append(k_a[n_p:].reshape(dec_batch, dec_seq, heads, hd))
        v_s.append(v_a[n_p:].reshape(dec_batch, dec_seq, heads, hd))
        s_p.append(sp)
        s_s.append(ss)

    y_prompt = x[:n_p].reshape(batch, lp, d)[:, N_META:]
    y_sample = x[n_p:].reshape(dec_batch, dec_seq, d)
    return (y_prompt, y_sample, jnp.stack(k_p), jnp.stack(v_p), jnp.stack(s_p),
            jnp.stack(k_s), jnp.stack(v_s), jnp.stack(s_s))
```

```python
import functools
import math

import numpy as np
import jax
import jax.numpy as jnp
from jax import lax
from jax.experimental import pallas as pl
from jax.experimental.pallas import tpu as pltpu

F32 = jnp.float32
BF16 = jnp.bfloat16

CHUNK = 64
N_META = 16
A_HD = 64
TOP_K = 4
SWIGLU_ALPHA = 1.702
SWIGLU_LIMIT = 7.0
EPS = 1e-6
NEG_INF = -1e30
REC_SUB = 16

LANES = 128
ATT_BLOCK = 2 * CHUNK
ATT_KEY_COLS = 256
ATT_TABLE_COLS = 512
MXU_DIM = 256
VMEM_LIMIT = 48 * 1024 * 1024


def _pick_block(n, candidates):
    for c in candidates:
        if n % c == 0:
            return c
    raise ValueError(f"no block size in {candidates} divides {n}")


def _dot(a, b, **kw):
    return jnp.dot(a, b, preferred_element_type=F32, **kw)


def _dot_nt(a, b):
    return lax.dot_general(a, b, (((1,), (1,)), ((), ())), preferred_element_type=F32)


def _dot_tn(a, b):
    return lax.dot_general(a, b, (((0,), (0,)), ((), ())), preferred_element_type=F32)


def _params(*sem):
    return pltpu.CompilerParams(dimension_semantics=sem, vmem_limit_bytes=VMEM_LIMIT)


def _rmsnorm_kernel(x_ref, g_ref, o_ref):
    x = x_ref[...]
    ms = jnp.mean(x * x, axis=-1, keepdims=True)
    o_ref[...] = (x * lax.rsqrt(ms + EPS) * g_ref[...]).astype(o_ref.dtype)


def rmsnorm_rows(x, g, tm):
    n, d = x.shape
    return pl.pallas_call(
        _rmsnorm_kernel,
        out_shape=jax.ShapeDtypeStruct((n, d), BF16),
        grid=(n // tm,),
        in_specs=[pl.BlockSpec((tm, d), lambda i: (i, 0)),
                  pl.BlockSpec((1, d), lambda i: (0, 0))],
        out_specs=pl.BlockSpec((tm, d), lambda i: (i, 0)),
        compiler_params=_params("parallel"),
        name="rmsnorm_rows",
    )(x, g.reshape(1, d))


def _proj_kernel(h_ref, w_ref, o_ref):
    o_ref[...] = _dot(h_ref[...], w_ref[...]).astype(o_ref.dtype)


def _proj_groupnorm_kernel(h_ref, w_ref, gsum_ref, gexp_ref, gain_ref, o_ref):
    z = _dot(h_ref[...], w_ref[...])
    ms = _dot((z * z).astype(BF16), gsum_ref[...])
    inv = lax.rsqrt(ms + EPS)
    inv_hi = inv.astype(BF16)
    inv_lo = (inv - inv_hi.astype(F32)).astype(BF16)
    inv_full = _dot(inv_hi, gexp_ref[...]) + _dot(inv_lo, gexp_ref[...])
    o_ref[...] = (z * inv_full * gain_ref[...]).astype(o_ref.dtype)


def project(h, w, out_dtype, tm, tn, group_gain=None, group=A_HD):
    n, k = h.shape
    m = w.shape[1]
    grid = (m // tn, n // tm)
    h_spec = pl.BlockSpec((tm, k), lambda j, i: (i, 0))
    w_spec = pl.BlockSpec((k, tn), lambda j, i: (0, j))
    o_spec = pl.BlockSpec((tm, tn), lambda j, i: (i, j))
    out_shape = jax.ShapeDtypeStruct((n, m), out_dtype)
    if group_gain is None:
        return pl.pallas_call(
            _proj_kernel, out_shape=out_shape, grid=grid,
            in_specs=[h_spec, w_spec], out_specs=o_spec,
            compiler_params=_params("parallel", "parallel"), name="project",
        )(h, w)
    ng = tn // group
    cols = np.arange(tn) // group
    gsum = jnp.asarray((cols[:, None] == np.arange(ng)[None, :]) / group, BF16)
    gexp = jnp.asarray(np.arange(ng)[:, None] == cols[None, :], BF16)
    return pl.pallas_call(
        _proj_groupnorm_kernel, out_shape=out_shape, grid=grid,
        in_specs=[h_spec, w_spec,
                  pl.BlockSpec((tn, ng), lambda j, i: (0, 0)),
                  pl.BlockSpec((ng, tn), lambda j, i: (0, 0)),
                  pl.BlockSpec((1, tn), lambda j, i: (0, j))],
        out_specs=o_spec,
        compiler_params=_params("parallel", "parallel"), name="project_groupnorm",
    )(h, w, gsum, gexp, group_gain.reshape(1, m).astype(F32))


def _map_masks(q):
    lane = lax.broadcasted_iota(jnp.int32, q.shape, 1)
    zero = jnp.zeros_like(q)
    return (jnp.where(lane < A_HD, q, zero), jnp.where(lane >= A_HD, q, zero))


def _subln(o, g_ref, post_scale):
    ms = jnp.mean(o * o, axis=-1, keepdims=True)
    return o * lax.rsqrt(ms + EPS) * g_ref[...] * post_scale


def _attn_prompt_kernel(lam_ref, slope_ref, q_ref, k_ref, v_ref, g_ref, o_ref, kb, vb, bias_tab, s_scr,
                        *, seq_len, post_scale):
    tb = ATT_BLOCK
    n_frames = seq_len - N_META
    n_blocks = n_frames // tb
    slope = slope_ref[pl.program_id(1)]
    lam = lam_ref[0]
    kb[...] = k_ref[...].astype(BF16)
    vb[...] = v_ref[...].astype(BF16)

    qm = _map_masks(q_ref[0:N_META, :])
    r = lax.broadcasted_iota(jnp.int32, (N_META, N_META), 0)
    c = lax.broadcasted_iota(jnp.int32, (N_META, N_META), 1)
    bias = -slope * jnp.abs(r - c).astype(F32)
    km = kb[0:N_META, :]
    vm = vb[0:N_META, :]
    ws = []
    for m in range(2):
        s = _dot_nt(qm[m], km) + bias
        p = jnp.exp(s - jnp.max(s, axis=-1, keepdims=True))
        ws.append(p / jnp.sum(p, axis=-1, keepdims=True))
    w = ws[0] - lam * ws[1]
    o_ref[0:N_META, :] = _subln(_dot(w.astype(BF16), vm), g_ref, post_scale).astype(o_ref.dtype)

    for c0 in range(0, n_frames, ATT_TABLE_COLS):
        wd = min(ATT_TABLE_COLS, n_frames - c0)
        r = lax.broadcasted_iota(jnp.int32, (tb, wd), 0)
        u = lax.broadcasted_iota(jnp.int32, (tb, wd), 1) + (c0 - (n_frames - tb))
        hidden = u >= (r // CHUNK + 1) * CHUNK
        bias_tab[:, c0:c0 + wd] = jnp.where(hidden, NEG_INF, -slope * jnp.abs(r - u).astype(F32))
    rm = lax.broadcasted_iota(jnp.int32, (tb, N_META), 0)
    cm = lax.broadcasted_iota(jnp.int32, (tb, N_META), 1)
    meta_bias0 = -slope * (N_META + rm - cm).astype(F32)

    def lane_tiles(x):
        return [x[:, t * LANES:(t + 1) * LANES] for t in range(x.shape[1] // LANES)]

    for i in range(n_blocks):
        q0 = N_META + i * tb
        qm = _map_masks(q_ref[q0:q0 + tb, :])
        n_keys = tb * (i + 1)
        off = n_frames - n_keys
        chunks = [(c0, min(ATT_KEY_COLS, n_keys - c0)) for c0 in range(0, n_keys, ATT_KEY_COLS)]
        outs = []
        for m in range(2):
            scr = s_scr.at[i % 2, m]
            s_meta = _dot_nt(qm[m], km) + (meta_bias0 - slope * float(tb * i))
            mx = None
            for c0, wd in chunks:
                s = (_dot_nt(qm[m], kb[N_META + c0:N_META + c0 + wd, :])
                     + bias_tab[:, off + c0:off + c0 + wd])
                scr[:, c0:c0 + wd] = s
                for t in lane_tiles(s):
                    mx = t if mx is None else jnp.maximum(mx, t)
            m_row = jnp.maximum(jnp.max(mx, axis=-1, keepdims=True),
                                jnp.max(s_meta, axis=-1, keepdims=True))
            p_meta = jnp.exp(s_meta - m_row)
            acc = _dot(p_meta.astype(BF16), vm)
            lsum = None
            for c0, wd in chunks:
                p = jnp.exp(scr[:, c0:c0 + wd] - m_row)
                for t in lane_tiles(p):
                    lsum = t if lsum is None else lsum + t
                acc = acc + _dot(p.astype(BF16), vb[N_META + c0:N_META + c0 + wd, :])
            l_row = jnp.sum(lsum, axis=-1, keepdims=True) + jnp.sum(p_meta, axis=-1, keepdims=True)
            outs.append(acc / l_row)
        o = outs[0] - lam * outs[1]
        o_ref[q0:q0 + tb, :] = _subln(o, g_ref, post_scale).astype(o_ref.dtype)


def attention_prompt(q, k, v, lam, slopes, subln_g, batch, seq_len, heads, row_block0, post_scale):
    hw = 2 * A_HD
    spec = lambda: pl.BlockSpec((seq_len, hw), lambda b, h, *_: (row_block0 + b, h))
    kern = functools.partial(_attn_prompt_kernel, seq_len=seq_len, post_scale=post_scale)
    return pl.pallas_call(
        kern,
        out_shape=jax.ShapeDtypeStruct((batch * seq_len, heads * hw), BF16),
        grid_spec=pltpu.PrefetchScalarGridSpec(
            num_scalar_prefetch=2, grid=(batch, heads),
            in_specs=[spec(), spec(), spec(), pl.BlockSpec((1, hw), lambda b, h, *_: (0, 0))],
            out_specs=pl.BlockSpec((seq_len, hw), lambda b, h, *_: (b, h)),
            scratch_shapes=[pltpu.VMEM((seq_len, hw), BF16), pltpu.VMEM((seq_len, hw), BF16),
                            pltpu.VMEM((ATT_BLOCK, seq_len - N_META), F32),
                            pltpu.VMEM((2, 2, ATT_BLOCK, seq_len - N_META), F32)]),
        compiler_params=_params("parallel", "parallel"), name="attention_prompt",
    )(lam, slopes, q, k, v, subln_g.reshape(1, hw))


def _attn_sample_kernel(lam_ref, slope_ref, q_ref, k_ref, v_ref, ck_ref, cv_ref, g_ref, o_ref,
                        *, post_scale):
    slope = slope_ref[pl.program_id(1)]
    lam = lam_ref[0]
    t = q_ref.shape[0]
    past = ck_ref.shape[1]
    qm = _map_masks(q_ref[...])
    kn = k_ref[...].astype(BF16)
    vn = v_ref[...].astype(BF16)
    kc = ck_ref[0].astype(BF16)
    vc = cv_ref[0].astype(BF16)
    r = lax.broadcasted_iota(jnp.int32, (t, past), 0)
    c = lax.broadcasted_iota(jnp.int32, (t, past), 1)
    bias_c = -slope * (past + r - c).astype(F32)
    r = lax.broadcasted_iota(jnp.int32, (t, t), 0)
    c = lax.broadcasted_iota(jnp.int32, (t, t), 1)
    bias_n = -slope * jnp.abs(r - c).astype(F32)
    wc, wn = [], []
    for m in range(2):
        sc = _dot_nt(qm[m], kc) + bias_c
        sn = _dot_nt(qm[m], kn) + bias_n
        mx = jnp.maximum(jnp.max(sc, axis=-1, keepdims=True), jnp.max(sn, axis=-1, keepdims=True))
        pc = jnp.exp(sc - mx)
        pn = jnp.exp(sn - mx)
        den = jnp.sum(pc, axis=-1, keepdims=True) + jnp.sum(pn, axis=-1, keepdims=True)
        wc.append(pc / den)
        wn.append(pn / den)
    o = (_dot((wc[0] - lam * wc[1]).astype(BF16), vc)
         + _dot((wn[0] - lam * wn[1]).astype(BF16), vn))
    o_ref[...] = _subln(o, g_ref, post_scale).astype(o_ref.dtype)


def attention_sample(q, k, v, cache_k, cache_v, lam, slopes, subln_g, batch, t, heads, row_block0,
                     post_scale):
    hw = 2 * A_HD
    past = cache_k.shape[1]
    new = lambda: pl.BlockSpec((t, hw), lambda b, h, *_: (row_block0 + b, h))
    cache = lambda: pl.BlockSpec((1, past, hw), lambda b, h, *_: (b, 0, h))
    kern = functools.partial(_attn_sample_kernel, post_scale=post_scale)
    return pl.pallas_call(
        kern,
        out_shape=jax.ShapeDtypeStruct((batch * t, heads * hw), BF16),
        grid_spec=pltpu.PrefetchScalarGridSpec(
            num_scalar_prefetch=2, grid=(batch, heads),
            in_specs=[new(), new(), new(), cache(), cache(),
                      pl.BlockSpec((1, hw), lambda b, h, *_: (0, 0))],
            out_specs=pl.BlockSpec((t, hw), lambda b, h, *_: (b, h))),
        compiler_params=_params("parallel", "parallel"), name="attention_sample",
    )(lam, slopes, q, k, v, cache_k, cache_v, subln_g.reshape(1, hw))


def _hgrn_chunk(start, size, heads_per_step, refs, st_ref, consts):
    q_ref, f_ref, i_ref, g_ref, lb_ref, ng_ref, o_ref = refs
    tri = consts
    nsub = size // REC_SUB
    rows = pl.ds(start, size)
    for p in range(heads_per_step):
        lanes = slice(p * LANES, (p + 1) * LANES)
        lb = lb_ref[:, lanes]
        fg = lb + (1.0 - lb) * jax.nn.sigmoid(f_ref[rows, lanes])
        kk = 1.0 - fg
        logf = jnp.log(fg)
        qq = jax.nn.silu(q_ref[rows, lanes].astype(F32))
        vv = i_ref[rows, lanes].astype(BF16)
        lf_hi = logf.astype(BF16)
        lf_lo = (logf - lf_hi.astype(F32)).astype(BF16)
        b = _dot(tri, lf_hi) + _dot(tri, lf_lo)
        r_parts = [jnp.zeros((REC_SUB, LANES), F32)]
        for s in range(1, nsub):
            r_parts.append(jnp.broadcast_to(b[s * REC_SUB - 1:s * REC_SUB, :], (REC_SUB, LANES)))
        r_sub = r_parts[0] if nsub == 1 else jnp.concatenate(r_parts, axis=0)
        q_sub = qq * jnp.exp(b - r_sub)
        q_dec = (q_sub * jnp.exp(r_sub)).astype(BF16)
        st = st_ref[p]
        o_inter = _dot_nt(q_dec, st.astype(BF16))
        b_last = b[size - 1:size, :]
        k_end = (kk * jnp.exp(b_last - b)).astype(BF16)
        o_rows = []
        for i in range(nsub):
            hi = (i + 1) * REC_SUB
            lo = i * REC_SUB
            k_rel = (kk[0:hi] * jnp.exp(r_sub[lo:lo + 1, :] - b[0:hi])).astype(BF16)
            a = _dot_nt(q_sub[lo:hi].astype(BF16), k_rel)
            rr = lax.broadcasted_iota(jnp.int32, (REC_SUB, hi), 0)
            cc = lax.broadcasted_iota(jnp.int32, (REC_SUB, hi), 1)
            a = jnp.where(cc <= rr + lo, a, 0.0)
            o_rows.append(o_inter[lo:hi] + _dot(a.astype(BF16), vv[0:hi]))
        o = o_rows[0] if nsub == 1 else jnp.concatenate(o_rows, axis=0)
        ms = jnp.mean(o * o, axis=-1, keepdims=True)
        on = o * lax.rsqrt(ms + EPS) * ng_ref[...]
        gg = g_ref[rows, lanes].astype(F32)
        o_ref[rows, lanes] = (on * jax.nn.silu(gg)).astype(o_ref.dtype)
        st_ref[p] = st * jnp.exp(b_last) + _dot_tn(vv, k_end)


def _hgrn_consts(size):
    r = lax.broadcasted_iota(jnp.int32, (size, size), 0)
    c = lax.broadcasted_iota(jnp.int32, (size, size), 1)
    return jnp.where(c <= r, 1.0, 0.0).astype(BF16)


def _hgrn_kernel(*args, seq_len, heads_per_step, has_state):
    if has_state:
        q_ref, f_ref, i_ref, g_ref, lb_ref, ng_ref, s0_ref, o_ref, s_ref, st_ref = args
    else:
        q_ref, f_ref, i_ref, g_ref, lb_ref, ng_ref, o_ref, s_ref, st_ref = args
    refs = (q_ref, f_ref, i_ref, g_ref, lb_ref, ng_ref, o_ref)
    for p in range(heads_per_step):
        if has_state:
            st_ref[p] = s0_ref[0, p].T
        else:
            st_ref[p] = jnp.zeros(st_ref.shape[1:], F32)
    _hgrn_chunk(0, N_META, heads_per_step, refs, st_ref, _hgrn_consts(N_META))
    n_chunks = (seq_len - N_META) // CHUNK
    if n_chunks:
        consts = _hgrn_consts(CHUNK)

        def body(ci, _):
            start = pl.multiple_of(N_META + ci * CHUNK, N_META)
            _hgrn_chunk(start, CHUNK, heads_per_step, refs, st_ref, consts)
            return 0

        lax.fori_loop(0, n_chunks, body, 0, unroll=2 if n_chunks % 2 == 0 else 1)
    for p in range(heads_per_step):
        s_ref[0, p] = st_ref[p].T


def hgrn2(zr, f_b, lb, norm_g, s0, batch, seq_len, heads, row_block0, heads_per_step):
    dk = LANES
    hpw = heads_per_step * dk
    d = heads * dk
    nhp = heads // heads_per_step
    col = lambda seg: pl.BlockSpec((seq_len, hpw), lambda b, hp, seg=seg: (row_block0 + b, seg * nhp + hp))
    in_specs = [col(0), pl.BlockSpec((seq_len, hpw), lambda b, hp: (row_block0 + b, hp)), col(1), col(2),
                pl.BlockSpec((1, hpw), lambda b, hp: (0, hp)),
                pl.BlockSpec((1, dk), lambda b, hp: (0, 0))]
    args = [zr, f_b, zr, zr, lb.reshape(1, d), norm_g.reshape(1, dk)]
    state_spec = pl.BlockSpec((1, heads_per_step, dk, dk), lambda b, hp: (b, hp, 0, 0))
    if s0 is not None:
        in_specs.append(state_spec)
        args.append(s0)
    kern = functools.partial(_hgrn_kernel, seq_len=seq_len, heads_per_step=heads_per_step,
                             has_state=s0 is not None)
    return pl.pallas_call(
        kern,
        out_shape=(jax.ShapeDtypeStruct((batch * seq_len, d), BF16),
                   jax.ShapeDtypeStruct((batch, heads, dk, dk), F32)),
        grid=(batch, nhp),
        in_specs=in_specs,
        out_specs=(pl.BlockSpec((seq_len, hpw), lambda b, hp: (b, hp)), state_spec),
        scratch_shapes=[pltpu.VMEM((heads_per_step, dk, dk), F32)],
        compiler_params=_params("parallel", "parallel"), name="hgrn2",
    )(*args)


def _merge_kernel(oa_ref, ob_ref, ga_ref, gb_ref, x_ref, wa_ref, wb_ref, wo_ref, g2_ref, wr_ref, br_ref,
                  x1_ref, h2_ref, lg_ref):
    ya = _dot(oa_ref[...], wa_ref[...])
    yb = _dot(ob_ref[...], wb_ref[...])
    mix = jax.nn.sigmoid(ga_ref[...].astype(F32)) * ya + jax.nn.sigmoid(gb_ref[...].astype(F32)) * yb
    x1 = x_ref[...] + _dot(mix.astype(BF16), wo_ref[...])
    x1_ref[...] = x1
    ms = jnp.mean(x1 * x1, axis=-1, keepdims=True)
    h2 = x1 * lax.rsqrt(ms + EPS) * g2_ref[...]
    h2_ref[...] = h2
    lg_ref[...] = _dot(h2, wr_ref[...], precision=lax.Precision.HIGHEST) + br_ref[...]


def merge_branches(oa, ob, zr, x, wa, wb, wo, g2, w_router, b_router, tm):
    n, d = x.shape
    ne = w_router.shape[1]
    row = lambda: pl.BlockSpec((tm, d), lambda i: (i, 0))
    full = lambda a: pl.BlockSpec(a.shape, lambda i: (0, 0))
    g2 = g2.reshape(1, d)
    b_router = b_router.reshape(1, ne)
    return pl.pallas_call(
        _merge_kernel,
        out_shape=(jax.ShapeDtypeStruct((n, d), F32), jax.ShapeDtypeStruct((n, d), F32),
                   jax.ShapeDtypeStruct((n, ne), F32)),
        grid=(n // tm,),
        in_specs=[row(), row(),
                  pl.BlockSpec((tm, d), lambda i: (i, 3)), pl.BlockSpec((tm, d), lambda i: (i, 4)),
                  row(), full(wa), full(wb), full(wo), full(g2), full(w_router), full(b_router)],
        out_specs=(row(), row(), pl.BlockSpec((tm, ne), lambda i: (i, 0))),
        compiler_params=_params("parallel"), name="merge_branches",
    )(oa, ob, zr, zr, x, wa, wb, wo, g2, w_router, b_router)


def _deinterleave_kernel(w_ref, perm_ref, g_ref, u_ref):
    perm = perm_ref[...]
    half = MXU_DIM // 2
    for t in range(w_ref.shape[2] // MXU_DIM):
        y = _dot(w_ref[0, :, t * MXU_DIM:(t + 1) * MXU_DIM].astype(BF16), perm)
        g_ref[0, :, t * half:(t + 1) * half] = y[:, :half].astype(BF16)
        u_ref[0, :, t * half:(t + 1) * half] = y[:, half:].astype(BF16)


def deinterleave_gate_up(w1, rows):
    e, k, n2 = w1.shape
    half = MXU_DIM // 2
    idx = np.arange(MXU_DIM)
    perm = np.zeros((MXU_DIM, MXU_DIM), np.float32)
    perm[idx, np.where(idx % 2 == 0, idx // 2, half + idx // 2)] = 1.0
    out = jax.ShapeDtypeStruct((e, k, n2 // 2), BF16)
    out_spec = lambda: pl.BlockSpec((1, rows, n2 // 2), lambda i, j: (i, j, 0))
    return pl.pallas_call(
        _deinterleave_kernel, out_shape=(out, out), grid=(e, k // rows),
        in_specs=[pl.BlockSpec((1, rows, n2), lambda i, j: (i, j, 0)),
                  pl.BlockSpec((MXU_DIM, MXU_DIM), lambda i, j: (0, 0))],
        out_specs=(out_spec(), out_spec()),
        compiler_params=_params("parallel", "parallel"), name="deinterleave_gate_up",
    )(w1, jnp.asarray(perm, BF16))


def _row_copy(src_hbm, row, dst, r, sem):
    return pltpu.make_async_copy(src_hbm.at[pl.ds(row, 1)], dst.at[pl.ds(r, 1)], sem)


def _moe_ffn_kernel(blk_e_ref, tok_ref, tok_next_ref, h_hbm, w1g_ref, w1u_ref, b1g_ref, b1u_ref,
                    w2_ref, b2_ref, o_ref, xbuf, sem):
    i = pl.program_id(0)
    nb = pl.num_programs(0)
    tm = xbuf.shape[1]
    slot = lax.rem(i, 2)

    def start_rows(ids_ref, s):
        def body(r, _):
            _row_copy(h_hbm, ids_ref[0, 0, r], xbuf.at[s], r, sem.at[s]).start()
            return 0
        lax.fori_loop(0, tm, body, 0, unroll=8)

    @pl.when(i == 0)
    def _():
        start_rows(tok_ref, 0)

    @pl.when(i + 1 < nb)
    def _():
        start_rows(tok_next_ref, 1 - slot)

    def wait_body(r, _):
        _row_copy(h_hbm, 0, xbuf.at[slot], r, sem.at[slot]).wait()
        return 0
    lax.fori_loop(0, tm, wait_body, 0, unroll=8)

    x = xbuf[slot].astype(BF16)
    gate = jnp.minimum(_dot(x, w1g_ref[0]) + b1g_ref[0], SWIGLU_LIMIT)
    up = jnp.clip(_dot(x, w1u_ref[0]) + b1u_ref[0], -SWIGLU_LIMIT, SWIGLU_LIMIT)
    act = gate * jax.nn.sigmoid(SWIGLU_ALPHA * gate) * (up + 1.0)
    o_ref[...] = _dot(act.astype(BF16), w2_ref[0]) + b2_ref[0]


def moe_ffn(h2, tok_blocks, blk_e, w1g, w1u, b1g, b1u, w2, b2, tm):
    n_blocks = tok_blocks.shape[0]
    d = h2.shape[1]
    dff = w2.shape[1]
    wspec = lambda a: pl.BlockSpec((1,) + a.shape[1:], lambda i, e: (e[i], 0, 0))
    ids = lambda f: pl.BlockSpec((1, 1, tm), f, memory_space=pltpu.SMEM)
    return pl.pallas_call(
        _moe_ffn_kernel,
        out_shape=jax.ShapeDtypeStruct((n_blocks * tm, d), F32),
        grid_spec=pltpu.PrefetchScalarGridSpec(
            num_scalar_prefetch=1, grid=(n_blocks,),
            in_specs=[ids(lambda i, e: (i, 0, 0)),
                      ids(lambda i, e: (jnp.minimum(i + 1, n_blocks - 1), 0, 0)),
                      pl.BlockSpec(memory_space=pl.ANY),
                      wspec(w1g), wspec(w1u), wspec(b1g), wspec(b1u), wspec(w2), wspec(b2)],
            out_specs=pl.BlockSpec((tm, d), lambda i, e: (i, 0)),
            scratch_shapes=[pltpu.VMEM((2, tm, d), F32), pltpu.SemaphoreType.DMA((2,))]),
        compiler_params=_params("arbitrary"), name="moe_ffn",
    )(blk_e, tok_blocks, tok_blocks, h2, w1g, w1u, b1g, b1u, w2, b2)


def _moe_combine_kernel(slot_ref, slot_next_ref, gate_ref, x_ref, rows_hbm, o_ref, buf, sem):
    i = pl.program_id(0)
    nb = pl.num_programs(0)
    tm = buf.shape[2]
    slot = lax.rem(i, 2)

    def start_rows(ids_ref, s):
        def body(r, _):
            for k in range(TOP_K):
                _row_copy(rows_hbm, ids_ref[0, 0, k * tm + r], buf.at[s, k], r, sem.at[s]).start()
            return 0
        lax.fori_loop(0, tm, body, 0, unroll=4)

    @pl.when(i == 0)
    def _():
        start_rows(slot_ref, 0)

    @pl.when(i + 1 < nb)
    def _():
        start_rows(slot_next_ref, 1 - slot)

    def wait_body(r, _):
        for k in range(TOP_K):
            _row_copy(rows_hbm, 0, buf.at[slot, k], r, sem.at[slot]).wait()
        return 0
    lax.fori_loop(0, tm, wait_body, 0, unroll=4)

    g = gate_ref[...]
    y = buf[slot, 0] * g[:, 0:1]
    for k in range(1, TOP_K):
        y = y + buf[slot, k] * g[:, k:k + 1]
    o_ref[...] = x_ref[...] + y


def moe_combine(rows, slots, gates, x1, tm):
    n, d = x1.shape
    nb = n // tm
    ids_arr = slots.reshape(nb, tm, TOP_K).transpose(0, 2, 1).reshape(nb, 1, TOP_K * tm)
    ids = lambda f: pl.BlockSpec((1, 1, TOP_K * tm), f, memory_space=pltpu.SMEM)
    return pl.pallas_call(
        _moe_combine_kernel,
        out_shape=jax.ShapeDtypeStruct((n, d), F32),
        grid=(nb,),
        in_specs=[ids(lambda i: (i, 0, 0)),
                  ids(lambda i: (jnp.minimum(i + 1, nb - 1), 0, 0)),
                  pl.BlockSpec((tm, TOP_K), lambda i: (i, 0)),
                  pl.BlockSpec((tm, d), lambda i: (i, 0)),
                  pl.BlockSpec(memory_space=pl.ANY)],
        out_specs=pl.BlockSpec((tm, d), lambda i: (i, 0)),
        scratch_shapes=[pltpu.VMEM((2, TOP_K, tm, d), F32), pltpu.SemaphoreType.DMA((2,))],
        compiler_params=_params("arbitrary"), name="moe_combine",
    )(ids_arr, ids_arr, gates, x1, rows)


def _route(logits, n_experts, tm):
    n = logits.shape[0]
    nk = n * TOP_K
    top_val, top_idx = lax.top_k(logits, TOP_K)
    gates = jax.nn.softmax(top_val, axis=-1)
    eid = top_idx.reshape(-1).astype(jnp.int32)
    tid = jnp.repeat(jnp.arange(n, dtype=jnp.int32), TOP_K)
    order = jnp.argsort(eid)
    e_s = eid[order]
    counts = jnp.bincount(eid, length=n_experts).astype(jnp.int32)
    pcounts = (counts + tm - 1) // tm * tm
    pend = jnp.cumsum(pcounts)
    pstart = pend - pcounts
    ustart = jnp.cumsum(counts) - counts
    slot_sorted = pstart[e_s] + jnp.arange(nk, dtype=jnp.int32) - ustart[e_s]
    n_blocks = -(-nk // tm) + n_experts
    tok_buf = jnp.zeros((n_blocks * tm,), jnp.int32).at[slot_sorted].set(tid[order])
    slots = jnp.zeros((nk,), jnp.int32).at[order].set(slot_sorted).reshape(n, TOP_K)
    blk_e = jnp.minimum(jnp.searchsorted(pend, jnp.arange(n_blocks, dtype=jnp.int32) * tm, side='right'),
                        n_experts - 1).astype(jnp.int32)
    return gates, slots, tok_buf.reshape(n_blocks, 1, tm), blk_e


def kernel(x_prompt, x_sample, cache_k, cache_v, state_hgrn, meta_tokens, norm_mix_g, w_in, q_norm_g,
           k_norm_g, lambda_q1, lambda_k1, lambda_q2, lambda_k2, subln_g, hgrn_lb, hgrn_norm_g, w_a, w_b,
           w_o, norm_ffn_g, w_router, b_router, w_mlp1, b_mlp1, w_mlp2, b_mlp2):
    batch, seq, d = x_prompt.shape
    dec_batch, dec_seq, _ = x_sample.shape
    depth = w_in.shape[0]
    heads = cache_k.shape[3]
    b_heads = state_hgrn.shape[2]
    n_experts = w_router.shape[-1]
    d_ff = w_mlp2.shape[2]
    lp = N_META + seq
    n_p = batch * lp
    n_s = dec_batch * dec_seq
    n = n_p + n_s
    assert dec_seq == N_META and seq % ATT_BLOCK == 0 and n_p % dec_seq == 0
    assert d == heads * 2 * A_HD == b_heads * LANES

    meta = jnp.broadcast_to(meta_tokens[None].astype(x_prompt.dtype), (batch, N_META, d))
    x = jnp.concatenate([jnp.concatenate([meta, x_prompt], axis=1).reshape(n_p, d),
                         x_sample.reshape(n_s, d)], axis=0)

    tm = _pick_block(n, (512, 256, 128, 64, 32, 16))
    tm_moe = 256 if n * TOP_K >= 256 * n_experts else 64
    tm_comb = _pick_block(n, (128, 64, 32, 16))
    slopes = jnp.asarray(np.power(2.0, -8.0 * np.arange(1, heads + 1) / heads).astype(np.float32))
    lower_bounds = jnp.cumsum(jax.nn.softmax(hgrn_lb.astype(F32), axis=0), axis=0)

    k_p, v_p, s_p, k_s, v_s, s_s = [], [], [], [], [], []
    for l in range(depth):
        lam_init = 0.8 - 0.6 * math.exp(-0.3 * l)
        lam = (jnp.exp(jnp.sum(lambda_q1[l].astype(F32) * lambda_k1[l].astype(F32)))
               - jnp.exp(jnp.sum(lambda_q2[l].astype(F32) * lambda_k2[l].astype(F32))) + lam_init).reshape(1)
        w = w_in[l].astype(BF16)
        seg = lambda j: w[:, j * d:(j + 1) * d]

        h = rmsnorm_rows(x, norm_mix_g[l], tm)
        q_a = project(h, seg(0), BF16, tm, d,
                      group_gain=jnp.tile(q_norm_g[l].astype(F32), d // A_HD) * (A_HD ** -0.5))
        k_a = project(h, seg(1), F32, tm, d, group_gain=jnp.tile(k_norm_g[l].astype(F32), d // A_HD))
        v_a = project(h, seg(2), F32, tm, d)
        f_b = project(h, seg(4), F32, tm, d)
        zr = project(h, jnp.concatenate([seg(3), seg(5), seg(6), seg(7), seg(8)], axis=1), BF16, tm, d)

        post = 1.0 - lam_init
        oa_p = attention_prompt(q_a, k_a, v_a, lam, slopes, subln_g[l], batch, lp, heads, 0, post)
        oa_s = attention_sample(q_a, k_a, v_a, cache_k[l].reshape(dec_batch, -1, d),
                                cache_v[l].reshape(dec_batch, -1, d), lam, slopes, subln_g[l],
                                dec_batch, dec_seq, heads, n_p // dec_seq, post)
        oa = jnp.concatenate([oa_p, oa_s], axis=0)

        ob_p, sp = hgrn2(zr, f_b, lower_bounds[l], hgrn_norm_g[l], None, batch, lp, b_heads, 0, 4)
        ob_s, ss = hgrn2(zr, f_b, lower_bounds[l], hgrn_norm_g[l], state_hgrn[l], dec_batch, dec_seq,
                         b_heads, n_p // dec_seq, 4)
        ob = jnp.concatenate([ob_p, ob_s], axis=0)

        x1, h2, logits = merge_branches(oa, ob, zr, x, w_a[l].astype(BF16), w_b[l].astype(BF16),
                                        w_o[l].astype(BF16), norm_ffn_g[l].astype(F32),
                                        w_router[l].astype(F32), b_router[l].astype(F32), tm)

        gates, slots, tok_blocks, blk_e = _route(logits, n_experts, tm_moe)
        w1g, w1u = deinterleave_gate_up(w_mlp1[l], _pick_block(d, (512, 256, 128)))
        rows = moe_ffn(h2, tok_blocks, blk_e, w1g, w1u,
                       b_mlp1[l][:, None, 0::2].astype(F32), b_mlp1[l][:, None, 1::2].astype(F32),
                       w_mlp2[l].astype(BF16), b_mlp2[l][:, None, :].astype(F32), tm_moe)
        x = moe_combine(rows, slots, gates, x1, tm_comb)

        hd = 2 * A_HD
        k_p.append(k_a[:n_p].reshape(batch, lp, heads, hd))
        v_p.append(v_a[:n_p].reshape(batch, lp, heads, hd))
        k_s.append(k_a[n_p:].reshape(dec_batch, dec_seq, heads, hd))
        v_s.append(v_a[n_p:].reshape(dec_batch, dec_seq, heads, hd))
        s_p.append(sp)
        s_s.append(ss)

    y_prompt = x[:n_p].reshape(batch, lp, d)[:, N_META:]
    y_sample = x[n_p:].reshape(dec_batch, dec_seq, d)
    return (y_prompt, y_sample, jnp.stack(k_p), jnp.stack(v_p), jnp.stack(s_p),
            jnp.stack(k_s), jnp.stack(v_s), jnp.stack(s_s))
```

```python
import functools
import math

import numpy as np
import jax
import jax.numpy as jnp
from jax import lax
from jax.experimental import pallas as pl
from jax.experimental.pallas import tpu as pltpu

F32 = jnp.float32
BF16 = jnp.bfloat16

CHUNK = 64
N_META = 16
A_HD = 64
TOP_K = 4
SWIGLU_ALPHA = 1.702
SWIGLU_LIMIT = 7.0
EPS = 1e-6
NEG_INF = -1e30
REC_SUB = 16

LANES = 128
SUBLANES = 8
ATT_BLOCK = 2 * CHUNK
ATT_KEY_COLS = 256
ATT_TABLE_COLS = 512
MXU_DIM = 256
PROJ_ROWS = 1040
VMEM_LIMIT = 48 * 1024 * 1024


def _largest_block(n, limit, multiple=16):
    for c in range(limit - limit % multiple, 0, -multiple):
        if n % c == 0:
            return c
    raise ValueError(f"no block of multiple {multiple} divides {n}")


def _pick_block(n, candidates):
    for c in candidates:
        if n % c == 0:
            return c
    raise ValueError(f"no block size in {candidates} divides {n}")


def _dot(a, b, **kw):
    return jnp.dot(a, b, preferred_element_type=F32, **kw)


def _dot_nt(a, b):
    return lax.dot_general(a, b, (((1,), (1,)), ((), ())), preferred_element_type=F32)


def _dot_tn(a, b):
    return lax.dot_general(a, b, (((0,), (0,)), ((), ())), preferred_element_type=F32)


def _store_row_tiles(ref, x):
    rows = x.shape[0]
    for s in range(SUBLANES):
        ref[pl.ds(s, rows, stride=SUBLANES), :] = x[:, s * LANES:(s + 1) * LANES]


def _load_row_tiles(ref, rows):
    return jnp.concatenate([ref[pl.ds(s, rows, stride=SUBLANES), :] for s in range(SUBLANES)], axis=1)


def _tile_copy(src, src_row, dst, dst_row, sem):
    return pltpu.make_async_copy(src.at[pl.ds(pl.multiple_of(src_row * SUBLANES, SUBLANES), SUBLANES)],
                                 dst.at[pl.ds(pl.multiple_of(dst_row * SUBLANES, SUBLANES), SUBLANES)], sem)


def _params(*sem):
    return pltpu.CompilerParams(dimension_semantics=sem, vmem_limit_bytes=VMEM_LIMIT)


def _rmsnorm_kernel(x_ref, g_ref, o_ref):
    x = x_ref[...]
    ms = jnp.mean(x * x, axis=-1, keepdims=True)
    o_ref[...] = (x * lax.rsqrt(ms + EPS) * g_ref[...]).astype(o_ref.dtype)


def rmsnorm_rows(x, g, tm):
    n, d = x.shape
    return pl.pallas_call(
        _rmsnorm_kernel,
        out_shape=jax.ShapeDtypeStruct((n, d), BF16),
        grid=(n // tm,),
        in_specs=[pl.BlockSpec((tm, d), lambda i: (i, 0)),
                  pl.BlockSpec((1, d), lambda i: (0, 0))],
        out_specs=pl.BlockSpec((tm, d), lambda i: (i, 0)),
        compiler_params=_params("parallel"),
        name="rmsnorm_rows",
    )(x, g.reshape(1, d))


def _proj_kernel(h_ref, w_ref, o_ref):
    o_ref[...] = _dot(h_ref[...], w_ref[...]).astype(o_ref.dtype)


def _proj_groupnorm_kernel(h_ref, w_ref, gsum_ref, gexp_ref, gain_ref, o_ref):
    z = _dot(h_ref[...], w_ref[...])
    ms = _dot((z * z).astype(BF16), gsum_ref[...])
    inv = lax.rsqrt(ms + EPS)
    inv_hi = inv.astype(BF16)
    inv_lo = (inv - inv_hi.astype(F32)).astype(BF16)
    inv_full = _dot(inv_hi, gexp_ref[...]) + _dot(inv_lo, gexp_ref[...])
    o_ref[...] = (z * inv_full * gain_ref[...]).astype(o_ref.dtype)


def project(h, w, out_dtype, tn, group_gain=None, group=A_HD, row0=0, n_rows=None):
    k = h.shape[1]
    n = h.shape[0] if n_rows is None else n_rows
    m = w.shape[1]
    tm = _largest_block(math.gcd(n, row0) if row0 else n, PROJ_ROWS)
    grid = (m // tn, n // tm)
    blk0 = row0 // tm
    h_spec = pl.BlockSpec((tm, k), lambda j, i: (blk0 + i, 0))
    w_spec = pl.BlockSpec((k, tn), lambda j, i: (0, j))
    o_spec = pl.BlockSpec((tm, tn), lambda j, i: (i, j))
    out_shape = jax.ShapeDtypeStruct((n, m), out_dtype)
    if group_gain is None:
        return pl.pallas_call(
            _proj_kernel, out_shape=out_shape, grid=grid,
            in_specs=[h_spec, w_spec], out_specs=o_spec,
            compiler_params=_params("parallel", "parallel"), name="project",
        )(h, w)
    ng = tn // group
    cols = np.arange(tn) // group
    gsum = jnp.asarray((cols[:, None] == np.arange(ng)[None, :]) / group, BF16)
    gexp = jnp.asarray(np.arange(ng)[:, None] == cols[None, :], BF16)
    return pl.pallas_call(
        _proj_groupnorm_kernel, out_shape=out_shape, grid=grid,
        in_specs=[h_spec, w_spec,
                  pl.BlockSpec((tn, ng), lambda j, i: (0, 0)),
                  pl.BlockSpec((ng, tn), lambda j, i: (0, 0)),
                  pl.BlockSpec((1, tn), lambda j, i: (0, j))],
        out_specs=o_spec,
        compiler_params=_params("parallel", "parallel"), name="project_groupnorm",
    )(h, w, gsum, gexp, group_gain.reshape(1, m).astype(F32))


def _map_masks(q):
    lane = lax.broadcasted_iota(jnp.int32, q.shape, 1)
    zero = jnp.zeros_like(q)
    return (jnp.where(lane < A_HD, q, zero), jnp.where(lane >= A_HD, q, zero))


def _subln(o, g_ref, post_scale):
    ms = jnp.mean(o * o, axis=-1, keepdims=True)
    return o * lax.rsqrt(ms + EPS) * g_ref[...] * post_scale


def _attn_prompt_kernel(lam_ref, slope_ref, q_ref, k_ref, v_ref, g_ref, o_ref, kb, vb, bias_tab, s_scr,
                        *, seq_len, post_scale):
    tb = ATT_BLOCK
    n_frames = seq_len - N_META
    n_blocks = n_frames // tb
    slope = slope_ref[pl.program_id(1)]
    lam = lam_ref[0]
    kb[...] = k_ref[...].astype(BF16)
    vb[...] = v_ref[...].astype(BF16)

    qm = _map_masks(q_ref[0:N_META, :])
    r = lax.broadcasted_iota(jnp.int32, (N_META, N_META), 0)
    c = lax.broadcasted_iota(jnp.int32, (N_META, N_META), 1)
    bias = -slope * jnp.abs(r - c).astype(F32)
    km = kb[0:N_META, :]
    vm = vb[0:N_META, :]
    ws = []
    for m in range(2):
        s = _dot_nt(qm[m], km) + bias
        p = jnp.exp(s - jnp.max(s, axis=-1, keepdims=True))
        ws.append(p / jnp.sum(p, axis=-1, keepdims=True))
    w = ws[0] - lam * ws[1]
    o_ref[0:N_META, :] = _subln(_dot(w.astype(BF16), vm), g_ref, post_scale).astype(o_ref.dtype)

    for c0 in range(0, n_frames, ATT_TABLE_COLS):
        wd = min(ATT_TABLE_COLS, n_frames - c0)
        r = lax.broadcasted_iota(jnp.int32, (tb, wd), 0)
        u = lax.broadcasted_iota(jnp.int32, (tb, wd), 1) + (c0 - (n_frames - tb))
        hidden = u >= (r // CHUNK + 1) * CHUNK
        bias_tab[:, c0:c0 + wd] = jnp.where(hidden, NEG_INF, -slope * jnp.abs(r - u).astype(F32))
    rm = lax.broadcasted_iota(jnp.int32, (tb, N_META), 0)
    cm = lax.broadcasted_iota(jnp.int32, (tb, N_META), 1)
    meta_bias0 = -slope * (N_META + rm - cm).astype(F32)

    def lane_tiles(x):
        return [x[:, t * LANES:(t + 1) * LANES] for t in range(x.shape[1] // LANES)]

    for i in range(n_blocks):
        q0 = N_META + i * tb
        qm = _map_masks(q_ref[q0:q0 + tb, :])
        n_keys = tb * (i + 1)
        off = n_frames - n_keys
        chunks = [(c0, min(ATT_KEY_COLS, n_keys - c0)) for c0 in range(0, n_keys, ATT_KEY_COLS)]
        outs = []
        for m in range(2):
            scr = s_scr.at[i % 2, m]
            s_meta = _dot_nt(qm[m], km) + (meta_bias0 - slope * float(tb * i))
            mx = None
            for c0, wd in chunks:
                s = (_dot_nt(qm[m], kb[N_META + c0:N_META + c0 + wd, :])
                     + bias_tab[:, off + c0:off + c0 + wd])
                scr[:, c0:c0 + wd] = s
                for t in lane_tiles(s):
                    mx = t if mx is None else jnp.maximum(mx, t)
            m_row = jnp.maximum(jnp.max(mx, axis=-1, keepdims=True),
                                jnp.max(s_meta, axis=-1, keepdims=True))
            p_meta = jnp.exp(s_meta - m_row)
            acc = _dot(p_meta.astype(BF16), vm)
            lsum = None
            for c0, wd in chunks:
                p = jnp.exp(scr[:, c0:c0 + wd] - m_row)
                for t in lane_tiles(p):
                    lsum = t if lsum is None else lsum + t
                acc = acc + _dot(p.astype(BF16), vb[N_META + c0:N_META + c0 + wd, :])
            l_row = jnp.sum(lsum, axis=-1, keepdims=True) + jnp.sum(p_meta, axis=-1, keepdims=True)
            outs.append(acc / l_row)
        o = outs[0] - lam * outs[1]
        o_ref[q0:q0 + tb, :] = _subln(o, g_ref, post_scale).astype(o_ref.dtype)


def attention_prompt(q, k, v, lam, slopes, subln_g, batch, seq_len, heads, row_block0, post_scale):
    hw = 2 * A_HD
    spec = lambda: pl.BlockSpec((seq_len, hw), lambda b, h, *_: (row_block0 + b, h))
    kern = functools.partial(_attn_prompt_kernel, seq_len=seq_len, post_scale=post_scale)
    return pl.pallas_call(
        kern,
        out_shape=jax.ShapeDtypeStruct((batch * seq_len, heads * hw), BF16),
        grid_spec=pltpu.PrefetchScalarGridSpec(
            num_scalar_prefetch=2, grid=(batch, heads),
            in_specs=[spec(), spec(), spec(), pl.BlockSpec((1, hw), lambda b, h, *_: (0, 0))],
            out_specs=pl.BlockSpec((seq_len, hw), lambda b, h, *_: (b, h)),
            scratch_shapes=[pltpu.VMEM((seq_len, hw), BF16), pltpu.VMEM((seq_len, hw), BF16),
                            pltpu.VMEM((ATT_BLOCK, seq_len - N_META), F32),
                            pltpu.VMEM((2, 2, ATT_BLOCK, seq_len - N_META), F32)]),
        compiler_params=_params("parallel", "parallel"), name="attention_prompt",
    )(lam, slopes, q, k, v, subln_g.reshape(1, hw))


def _attn_sample_kernel(lam_ref, slope_ref, q_ref, k_ref, v_ref, ck_ref, cv_ref, g_ref, o_ref,
                        *, post_scale):
    slope = slope_ref[pl.program_id(1)]
    lam = lam_ref[0]
    t = q_ref.shape[0]
    past = ck_ref.shape[1]
    qm = _map_masks(q_ref[...])
    kn = k_ref[...].astype(BF16)
    vn = v_ref[...].astype(BF16)
    kc = ck_ref[0].astype(BF16)
    vc = cv_ref[0].astype(BF16)
    r = lax.broadcasted_iota(jnp.int32, (t, past), 0)
    c = lax.broadcasted_iota(jnp.int32, (t, past), 1)
    bias_c = -slope * (past + r - c).astype(F32)
    r = lax.broadcasted_iota(jnp.int32, (t, t), 0)
    c = lax.broadcasted_iota(jnp.int32, (t, t), 1)
    bias_n = -slope * jnp.abs(r - c).astype(F32)
    wc, wn = [], []
    for m in range(2):
        sc = _dot_nt(qm[m], kc) + bias_c
        sn = _dot_nt(qm[m], kn) + bias_n
        mx = jnp.maximum(jnp.max(sc, axis=-1, keepdims=True), jnp.max(sn, axis=-1, keepdims=True))
        pc = jnp.exp(sc - mx)
        pn = jnp.exp(sn - mx)
        den = jnp.sum(pc, axis=-1, keepdims=True) + jnp.sum(pn, axis=-1, keepdims=True)
        wc.append(pc / den)
        wn.append(pn / den)
    o = (_dot((wc[0] - lam * wc[1]).astype(BF16), vc)
         + _dot((wn[0] - lam * wn[1]).astype(BF16), vn))
    o_ref[...] = _subln(o, g_ref, post_scale).astype(o_ref.dtype)


def attention_sample(q, k, v, cache_k, cache_v, lam, slopes, subln_g, batch, t, heads, row_block0,
                     post_scale):
    hw = 2 * A_HD
    past = cache_k.shape[1]
    new_q = pl.BlockSpec((t, hw), lambda b, h, *_: (row_block0 + b, h))
    new = lambda: pl.BlockSpec((t, hw), lambda b, h, *_: (b, h))
    cache = lambda: pl.BlockSpec((1, past, hw), lambda b, h, *_: (b, 0, h))
    kern = functools.partial(_attn_sample_kernel, post_scale=post_scale)
    return pl.pallas_call(
        kern,
        out_shape=jax.ShapeDtypeStruct((batch * t, heads * hw), BF16),
        grid_spec=pltpu.PrefetchScalarGridSpec(
            num_scalar_prefetch=2, grid=(batch, heads),
            in_specs=[new_q, new(), new(), cache(), cache(),
                      pl.BlockSpec((1, hw), lambda b, h, *_: (0, 0))],
            out_specs=pl.BlockSpec((t, hw), lambda b, h, *_: (b, h))),
        compiler_params=_params("parallel", "parallel"), name="attention_sample",
    )(lam, slopes, q, k, v, cache_k, cache_v, subln_g.reshape(1, hw))


def _hgrn_chunk(start, size, heads_per_step, refs, st_ref, consts):
    q_ref, f_ref, i_ref, g_ref, lb_ref, ng_ref, o_ref = refs
    tri = consts
    nsub = size // REC_SUB
    rows = pl.ds(start, size)
    row = lax.broadcasted_iota(jnp.int32, (size, LANES), 0)
    rr = lax.broadcasted_iota(jnp.int32, (size, size), 0)
    cc = lax.broadcasted_iota(jnp.int32, (size, size), 1)
    for p in range(heads_per_step):
        lanes = slice(p * LANES, (p + 1) * LANES)
        lb = lb_ref[:, lanes]
        fg = lb + (1.0 - lb) * jax.nn.sigmoid(f_ref[rows, lanes])
        kk = 1.0 - fg
        logf = jnp.log(fg)
        qq = jax.nn.silu(q_ref[rows, lanes].astype(F32))
        vv = i_ref[rows, lanes].astype(BF16)
        lf_hi = logf.astype(BF16)
        lf_lo = (logf - lf_hi.astype(F32)).astype(BF16)
        b = _dot(tri, lf_hi) + _dot(tri, lf_lo)
        r_parts = [jnp.zeros((REC_SUB, LANES), F32)]
        for s in range(1, nsub):
            r_parts.append(jnp.broadcast_to(b[s * REC_SUB - 1:s * REC_SUB, :], (REC_SUB, LANES)))
        r_sub = r_parts[0] if nsub == 1 else jnp.concatenate(r_parts, axis=0)
        q_sub = qq * jnp.exp(b - r_sub)
        q_dec = (q_sub * jnp.exp(r_sub)).astype(BF16)
        b_last = b[size - 1:size, :]
        k_end = (kk * jnp.exp(b_last - b)).astype(BF16)
        q_parts, k_parts = [], []
        for i in range(nsub):
            in_sub = (row >= i * REC_SUB) & (row < (i + 1) * REC_SUB)
            q_parts.append(jnp.where(in_sub, q_sub, 0.0).astype(BF16))
            rel = jnp.where(row < (i + 1) * REC_SUB, r_sub[i * REC_SUB:i * REC_SUB + 1, :] - b, NEG_INF)
            k_parts.append((kk * jnp.exp(rel)).astype(BF16))
        q_hat = q_parts[0] if nsub == 1 else jnp.concatenate(q_parts, axis=1)
        k_hat = k_parts[0] if nsub == 1 else jnp.concatenate(k_parts, axis=1)
        a = jnp.where(cc <= rr, _dot_nt(q_hat, k_hat), 0.0)
        st = st_ref[p]
        o = _dot(jnp.concatenate([q_dec, a.astype(BF16)], axis=1),
                 jnp.concatenate([st.astype(BF16), vv], axis=0))
        ms = jnp.mean(o * o, axis=-1, keepdims=True)
        on = o * lax.rsqrt(ms + EPS) * ng_ref[...]
        gg = g_ref[rows, lanes].astype(F32)
        o_ref[rows, lanes] = (on * jax.nn.silu(gg)).astype(o_ref.dtype)
        decay_cols = jnp.broadcast_to(jnp.exp(b_last), (LANES, LANES)).T
        st_ref[p] = st * decay_cols + _dot_tn(k_end, vv)


def _hgrn_consts(size):
    r = lax.broadcasted_iota(jnp.int32, (size, size), 0)
    c = lax.broadcasted_iota(jnp.int32, (size, size), 1)
    return jnp.where(c <= r, 1.0, 0.0).astype(BF16)


def _hgrn_kernel(*args, seq_len, heads_per_step, has_state):
    if has_state:
        q_ref, f_ref, i_ref, g_ref, lb_ref, ng_ref, s0_ref, o_ref, s_ref, st_ref = args
    else:
        q_ref, f_ref, i_ref, g_ref, lb_ref, ng_ref, o_ref, s_ref, st_ref = args
    refs = (q_ref, f_ref, i_ref, g_ref, lb_ref, ng_ref, o_ref)
    for p in range(heads_per_step):
        if has_state:
            st_ref[p] = s0_ref[0, p]
        else:
            st_ref[p] = jnp.zeros(st_ref.shape[1:], F32)
    _hgrn_chunk(0, N_META, heads_per_step, refs, st_ref, _hgrn_consts(N_META))
    n_chunks = (seq_len - N_META) // CHUNK
    if n_chunks:
        consts = _hgrn_consts(CHUNK)

        def body(ci, _):
            start = pl.multiple_of(N_META + ci * CHUNK, N_META)
            _hgrn_chunk(start, CHUNK, heads_per_step, refs, st_ref, consts)
            return 0

        lax.fori_loop(0, n_chunks, body, 0, unroll=2 if n_chunks % 2 == 0 else 1)
    for p in range(heads_per_step):
        s_ref[0, p] = st_ref[p]


def hgrn2(zr, f_b, lb, norm_g, s0, batch, seq_len, heads, row_block0, heads_per_step):
    dk = LANES
    hpw = heads_per_step * dk
    d = heads * dk
    nhp = heads // heads_per_step
    col = lambda seg: pl.BlockSpec((seq_len, hpw), lambda b, hp, seg=seg: (row_block0 + b, seg * nhp + hp))
    in_specs = [col(0), pl.BlockSpec((seq_len, hpw), lambda b, hp: (row_block0 + b, hp)), col(1), col(2),
                pl.BlockSpec((1, hpw), lambda b, hp: (0, hp)),
                pl.BlockSpec((1, dk), lambda b, hp: (0, 0))]
    args = [zr, f_b, zr, zr, lb.reshape(1, d), norm_g.reshape(1, dk)]
    state_spec = pl.BlockSpec((1, heads_per_step, dk, dk), lambda b, hp: (b, hp, 0, 0))
    if s0 is not None:
        in_specs.append(state_spec)
        args.append(s0)
    kern = functools.partial(_hgrn_kernel, seq_len=seq_len, heads_per_step=heads_per_step,
                             has_state=s0 is not None)
    return pl.pallas_call(
        kern,
        out_shape=(jax.ShapeDtypeStruct((batch * seq_len, d), BF16),
                   jax.ShapeDtypeStruct((batch, heads, dk, dk), F32)),
        grid=(batch, nhp),
        in_specs=in_specs,
        out_specs=(pl.BlockSpec((seq_len, hpw), lambda b, hp: (b, hp)), state_spec),
        scratch_shapes=[pltpu.VMEM((heads_per_step, dk, dk), F32)],
        compiler_params=_params("parallel", "parallel"), name="hgrn2",
    )(*args)


def _merge_kernel(oa_ref, ob_ref, ga_ref, gb_ref, x_ref, wa_ref, wb_ref, wo_ref, g2_ref, wr_ref, br_ref,
                  x1_ref, h2_ref, lg_ref):
    ya = _dot(oa_ref[...], wa_ref[...])
    yb = _dot(ob_ref[...], wb_ref[...])
    mix = jax.nn.sigmoid(ga_ref[...].astype(F32)) * ya + jax.nn.sigmoid(gb_ref[...].astype(F32)) * yb
    x1 = x_ref[...] + _dot(mix.astype(BF16), wo_ref[...])
    x1_ref[...] = x1
    ms = jnp.mean(x1 * x1, axis=-1, keepdims=True)
    h2 = x1 * lax.rsqrt(ms + EPS) * g2_ref[...]
    _store_row_tiles(h2_ref, h2)
    ne = lg_ref.shape[1]
    h_hi = h2.astype(BF16)
    h_lo = (h2 - h_hi.astype(F32)).astype(BF16)
    both = _dot(h_hi, wr_ref[...])
    lg_ref[...] = both[:, :ne] + both[:, ne:] + _dot(h_lo, wr_ref[:, :ne]) + br_ref[...]


def merge_branches(oa, ob, zr, x, wa, wb, wo, g2, w_router, b_router, tm):
    n, d = x.shape
    ne = w_router.shape[1]
    row = lambda: pl.BlockSpec((tm, d), lambda i: (i, 0))
    full = lambda a: pl.BlockSpec(a.shape, lambda i: (0, 0))
    g2 = g2.reshape(1, d)
    b_router = b_router.reshape(1, ne)
    w_hi = w_router.astype(BF16)
    w_router = jnp.concatenate([w_hi, (w_router - w_hi.astype(F32)).astype(BF16)], axis=1)
    return pl.pallas_call(
        _merge_kernel,
        out_shape=(jax.ShapeDtypeStruct((n, d), F32), jax.ShapeDtypeStruct((n * SUBLANES, LANES), F32),
                   jax.ShapeDtypeStruct((n, ne), F32)),
        grid=(n // tm,),
        in_specs=[row(), row(),
                  pl.BlockSpec((tm, d), lambda i: (i, 3)), pl.BlockSpec((tm, d), lambda i: (i, 4)),
                  row(), full(wa), full(wb), full(wo), full(g2), full(w_router), full(b_router)],
        out_specs=(row(), pl.BlockSpec((tm * SUBLANES, LANES), lambda i: (i, 0)),
                   pl.BlockSpec((tm, ne), lambda i: (i, 0))),
        compiler_params=_params("parallel"), name="merge_branches",
    )(oa, ob, zr, zr, x, wa, wb, wo, g2, w_router, b_router)


def _deinterleave_kernel(w_ref, perm_ref, g_ref, u_ref):
    perm = perm_ref[...]
    half = MXU_DIM // 2
    for t in range(w_ref.shape[2] // MXU_DIM):
        y = _dot(w_ref[0, :, t * MXU_DIM:(t + 1) * MXU_DIM].astype(BF16), perm)
        g_ref[0, :, t * half:(t + 1) * half] = y[:, :half].astype(BF16)
        u_ref[0, :, t * half:(t + 1) * half] = y[:, half:].astype(BF16)


def deinterleave_gate_up(w1, rows):
    e, k, n2 = w1.shape
    half = MXU_DIM // 2
    idx = np.arange(MXU_DIM)
    perm = np.zeros((MXU_DIM, MXU_DIM), np.float32)
    perm[idx, np.where(idx % 2 == 0, idx // 2, half + idx // 2)] = 1.0
    out = jax.ShapeDtypeStruct((e, k, n2 // 2), BF16)
    out_spec = lambda: pl.BlockSpec((1, rows, n2 // 2), lambda i, j: (i, j, 0))
    return pl.pallas_call(
        _deinterleave_kernel, out_shape=(out, out), grid=(e, k // rows),
        in_specs=[pl.BlockSpec((1, rows, n2), lambda i, j: (i, j, 0)),
                  pl.BlockSpec((MXU_DIM, MXU_DIM), lambda i, j: (0, 0))],
        out_specs=(out_spec(), out_spec()),
        compiler_params=_params("parallel", "parallel"), name="deinterleave_gate_up",
    )(w1, jnp.asarray(perm, BF16))


def _moe_dispatch_kernel(slot_ref, h_hbm, o_hbm, sem):
    i = pl.program_id(0)
    nb = pl.num_programs(0)
    tm = slot_ref.shape[2] // TOP_K
    par = lax.rem(i, 2)

    def copies(r, ids_ref, s):
        return [_tile_copy(h_hbm, i * tm + r, o_hbm, ids_ref[0, 0, k * tm + r], sem.at[s])
                for k in range(TOP_K)]

    def wait_all(s):
        def body(r, _):
            for k in range(TOP_K):
                _tile_copy(h_hbm, 0, o_hbm, 0, sem.at[s]).wait()
            return 0
        lax.fori_loop(0, tm, body, 0, unroll=4)

    @pl.when(i > 0)
    def _():
        wait_all(1 - par)

    def body(r, _):
        for k, c in enumerate(copies(r, slot_ref, par)):
            c.start(priority=k % 2)
        return 0
    lax.fori_loop(0, tm, body, 0, unroll=2)

    @pl.when(i == nb - 1)
    def _():
        wait_all(par)


def moe_dispatch(h2_tiles, slots, tm):
    n = slots.shape[0]
    nb = n // tm
    ids_arr = slots.reshape(nb, tm, TOP_K).transpose(0, 2, 1).reshape(nb, 1, TOP_K * tm)
    return pl.pallas_call(
        _moe_dispatch_kernel,
        out_shape=jax.ShapeDtypeStruct((n * TOP_K * SUBLANES, LANES), F32),
        grid=(nb,),
        in_specs=[pl.BlockSpec((1, 1, TOP_K * tm), lambda i: (i, 0, 0), memory_space=pltpu.SMEM),
                  pl.BlockSpec(memory_space=pl.ANY)],
        out_specs=pl.BlockSpec(memory_space=pl.ANY),
        scratch_shapes=[pltpu.SemaphoreType.DMA((2,))],
        compiler_params=_params("arbitrary"), name="moe_dispatch",
    )(ids_arr, h2_tiles)


def _moe_ffn_kernel(blk_ref, exp_ref, lo_ref, hi_ref, first_ref, x_ref, w1g_ref, w1u_ref, b1g_ref,
                    b1u_ref, w2_ref, b2_ref, o_ref):
    i = pl.program_id(0)
    tm = x_ref.shape[0] // SUBLANES
    lo = lo_ref[i]
    hi = hi_ref[i]

    @pl.when(hi > lo)
    def _():
        x = _load_row_tiles(x_ref, tm).astype(BF16)
        gate = jnp.minimum(_dot(x, w1g_ref[0]) + b1g_ref[0], SWIGLU_LIMIT)
        up = jnp.clip(_dot(x, w1u_ref[0]) + b1u_ref[0], -SWIGLU_LIMIT, SWIGLU_LIMIT)
        act = gate * jax.nn.sigmoid(SWIGLU_ALPHA * gate) * (up + 1.0)
        y = _dot(act.astype(BF16), w2_ref[0]) + b2_ref[0]
        row = lax.broadcasted_iota(jnp.int32, (tm, LANES), 0)
        mine = (row >= lo) & (row < hi)

        def write(keep_old):
            for s in range(SUBLANES):
                dst = (pl.ds(s, tm, stride=SUBLANES), slice(None))
                old = o_ref[dst] if keep_old else 0.0
                o_ref[dst] = jnp.where(mine, y[:, s * LANES:(s + 1) * LANES], old)

        pl.when(first_ref[i] == 1)(functools.partial(write, False))
        pl.when(first_ref[i] == 0)(functools.partial(write, True))


def moe_ffn(x_tiles, items, w1g, w1u, b1g, b1u, w2, b2, tm):
    n_items = items[0].shape[0]
    wspec = lambda a: pl.BlockSpec((1,) + a.shape[1:], lambda i, blk, exp, *_: (exp[i], 0, 0))
    rows = lambda: pl.BlockSpec((tm * SUBLANES, LANES), lambda i, blk, *_: (blk[i], 0))
    return pl.pallas_call(
        _moe_ffn_kernel,
        out_shape=jax.ShapeDtypeStruct(x_tiles.shape, F32),
        grid_spec=pltpu.PrefetchScalarGridSpec(
            num_scalar_prefetch=5, grid=(n_items,),
            in_specs=[rows(), wspec(w1g), wspec(w1u), wspec(b1g), wspec(b1u), wspec(w2), wspec(b2)],
            out_specs=rows()),
        compiler_params=_params("arbitrary"), name="moe_ffn",
    )(*items, x_tiles, w1g, w1u, b1g, b1u, w2, b2)


def _moe_combine_kernel(slot_ref, slot_next_ref, gate_ref, x_ref, rows_hbm, o_ref, buf, sem):
    i = pl.program_id(0)
    nb = pl.num_programs(0)
    tm = buf.shape[2] // SUBLANES
    slot = lax.rem(i, 2)

    def start_rows(ids_ref, s):
        def body(r, _):
            for k in range(TOP_K):
                _tile_copy(rows_hbm, ids_ref[0, 0, k * tm + r], buf.at[s, k], r,
                           sem.at[s]).start(priority=k % 2)
            return 0
        lax.fori_loop(0, tm, body, 0, unroll=2)

    @pl.when(i == 0)
    def _():
        start_rows(slot_ref, 0)

    @pl.when(i + 1 < nb)
    def _():
        start_rows(slot_next_ref, 1 - slot)

    def wait_body(r, _):
        for k in range(TOP_K):
            _tile_copy(rows_hbm, 0, buf.at[slot, k], r, sem.at[slot]).wait()
        return 0
    lax.fori_loop(0, tm, wait_body, 0, unroll=4)

    g = gate_ref[...]
    gk = [g[:, k:k + 1] for k in range(TOP_K)]
    for s in range(SUBLANES):
        piece = pl.ds(s, tm, stride=SUBLANES)
        y = buf[slot, 0, piece, :] * gk[0]
        for k in range(1, TOP_K):
            y = y + buf[slot, k, piece, :] * gk[k]
        o_ref[:, s * LANES:(s + 1) * LANES] = x_ref[:, s * LANES:(s + 1) * LANES] + y


def moe_combine(rows, slots, gates, x1, tm):
    n, d = x1.shape
    nb = n // tm
    ids_arr = slots.reshape(nb, tm, TOP_K).transpose(0, 2, 1).reshape(nb, 1, TOP_K * tm)
    ids = lambda f: pl.BlockSpec((1, 1, TOP_K * tm), f, memory_space=pltpu.SMEM)
    return pl.pallas_call(
        _moe_combine_kernel,
        out_shape=jax.ShapeDtypeStruct((n, d), F32),
        grid=(nb,),
        in_specs=[ids(lambda i: (i, 0, 0)),
                  ids(lambda i: (jnp.minimum(i + 1, nb - 1), 0, 0)),
                  pl.BlockSpec((tm, TOP_K), lambda i: (i, 0)),
                  pl.BlockSpec((tm, d), lambda i: (i, 0)),
                  pl.BlockSpec(memory_space=pl.ANY)],
        out_specs=pl.BlockSpec((tm, d), lambda i: (i, 0)),
        scratch_shapes=[pltpu.VMEM((2, TOP_K, tm * SUBLANES, LANES), F32), pltpu.SemaphoreType.DMA((2,))],
        compiler_params=_params("arbitrary"), name="moe_combine",
    )(ids_arr, ids_arr, gates, x1, rows)


def _route(logits, n_experts, tm, tb):
    n = logits.shape[0]
    nk = n * TOP_K
    i32 = jnp.int32
    top_val, top_idx = lax.top_k(logits, TOP_K)
    gates = jax.nn.softmax(top_val, axis=-1)
    eid = top_idx.astype(i32)
    chosen = (eid[:, :, None] == jnp.arange(n_experts, dtype=i32)[None, None, :]).any(axis=1)
    nb = n // tb
    cb = chosen.reshape(nb, tb, n_experts).astype(BF16)
    before = jnp.asarray(np.tril(np.ones((tb, tb), np.float32), -1), BF16)
    within = jnp.einsum('ts,bse->bte', before, cb, preferred_element_type=F32)
    per_block = jnp.sum(cb.astype(F32), axis=1)
    block_off = jnp.cumsum(per_block, axis=0) - per_block
    rank = (within + block_off[:, None, :]).reshape(n, n_experts)
    counts = jnp.sum(per_block, axis=0).astype(i32)
    gend = jnp.cumsum(counts)
    gstart = gend - counts
    slots = gstart[eid] + jnp.take_along_axis(rank, eid, axis=1).astype(i32)

    n_blk = nk // tm
    n_items = n_blk + n_experts - 1
    first_blk = gstart // tm
    tiles = jnp.where(counts > 0, (gend - 1) // tm - first_blk + 1, 0)
    tile_end = jnp.cumsum(tiles)
    idx = jnp.arange(n_items, dtype=i32)
    valid = idx < tile_end[-1]
    exp = jnp.minimum(jnp.searchsorted(tile_end, idx, side='right'), n_experts - 1).astype(i32)
    blk = jnp.where(valid, first_blk[exp] + idx - (tile_end[exp] - tiles[exp]), n_blk - 1)
    lo = jnp.where(valid, jnp.clip(gstart[exp] - blk * tm, 0, tm), 0)
    hi = jnp.where(valid, jnp.clip(gend[exp] - blk * tm, 0, tm), 0)
    first = (blk != jnp.concatenate([jnp.full((1,), -1, i32), blk[:-1]])).astype(i32)
    return gates, slots, (blk.astype(i32), exp, lo.astype(i32), hi.astype(i32), first)


def kernel(x_prompt, x_sample, cache_k, cache_v, state_hgrn, meta_tokens, norm_mix_g, w_in, q_norm_g,
           k_norm_g, lambda_q1, lambda_k1, lambda_q2, lambda_k2, subln_g, hgrn_lb, hgrn_norm_g, w_a, w_b,
           w_o, norm_ffn_g, w_router, b_router, w_mlp1, b_mlp1, w_mlp2, b_mlp2):
    batch, seq, d = x_prompt.shape
    dec_batch, dec_seq, _ = x_sample.shape
    depth = w_in.shape[0]
    heads = cache_k.shape[3]
    b_heads = state_hgrn.shape[2]
    n_experts = w_router.shape[-1]
    d_ff = w_mlp2.shape[2]
    lp = N_META + seq
    n_p = batch * lp
    n_s = dec_batch * dec_seq
    n = n_p + n_s
    assert dec_seq == N_META and seq % ATT_BLOCK == 0 and n_p % dec_seq == 0
    assert d == heads * 2 * A_HD == b_heads * LANES == SUBLANES * LANES

    meta = jnp.broadcast_to(meta_tokens[None].astype(x_prompt.dtype), (batch, N_META, d))
    x = jnp.concatenate([jnp.concatenate([meta, x_prompt], axis=1).reshape(n_p, d),
                         x_sample.reshape(n_s, d)], axis=0)

    tm = _pick_block(n, (512, 256, 128, 64, 32, 16))
    tm_moe = _pick_block(n * TOP_K, (256, 128, 64))
    tm_tok = _pick_block(n, (256, 128, 64, 32, 16))
    tm_comb = _pick_block(n, (128, 64, 32, 16))
    slopes = jnp.asarray(np.power(2.0, -8.0 * np.arange(1, heads + 1) / heads).astype(np.float32))
    lower_bounds = jnp.cumsum(jax.nn.softmax(hgrn_lb.astype(F32), axis=0), axis=0)

    k_p, v_p, s_p, k_s, v_s, s_s = [], [], [], [], [], []
    for l in range(depth):
        lam_init = 0.8 - 0.6 * math.exp(-0.3 * l)
        lam = (jnp.exp(jnp.sum(lambda_q1[l].astype(F32) * lambda_k1[l].astype(F32)))
               - jnp.exp(jnp.sum(lambda_q2[l].astype(F32) * lambda_k2[l].astype(F32))) + lam_init).reshape(1)
        w = w_in[l].astype(BF16)
        seg = lambda j: w[:, j * d:(j + 1) * d]

        h = rmsnorm_rows(x, norm_mix_g[l], tm)
        q_a = project(h, seg(0), BF16, d,
                      group_gain=jnp.tile(q_norm_g[l].astype(F32), d // A_HD) * (A_HD ** -0.5))
        k_gain = jnp.tile(k_norm_g[l].astype(F32), d // A_HD)
        k_pr = project(h, seg(1), F32, d, group_gain=k_gain, n_rows=n_p)
        k_sm = project(h, seg(1), F32, d, group_gain=k_gain, row0=n_p, n_rows=n_s)
        v_pr = project(h, seg(2), F32, d, n_rows=n_p)
        v_sm = project(h, seg(2), F32, d, row0=n_p, n_rows=n_s)
        f_b = project(h, seg(4), F32, d)
        zr = project(h, jnp.concatenate([seg(3), seg(5), seg(6), seg(7), seg(8)], axis=1), BF16, d)

        post = 1.0 - lam_init
        oa_p = attention_prompt(q_a, k_pr, v_pr, lam, slopes, subln_g[l], batch, lp, heads, 0, post)
        oa_s = attention_sample(q_a, k_sm, v_sm, cache_k[l].reshape(dec_batch, -1, d),
                                cache_v[l].reshape(dec_batch, -1, d), lam, slopes, subln_g[l],
                                dec_batch, dec_seq, heads, n_p // dec_seq, post)
        oa = jnp.concatenate([oa_p, oa_s], axis=0)

        ob_p, sp = hgrn2(zr, f_b, lower_bounds[l], hgrn_norm_g[l], None, batch, lp, b_heads, 0, 4)
        ob_s, ss = hgrn2(zr, f_b, lower_bounds[l], hgrn_norm_g[l], state_hgrn[l], dec_batch, dec_seq,
                         b_heads, n_p // dec_seq, 4)
        ob = jnp.concatenate([ob_p, ob_s], axis=0)

        x1, h2, logits = merge_branches(oa, ob, zr, x, w_a[l].astype(BF16), w_b[l].astype(BF16),
                                        w_o[l].astype(BF16), norm_ffn_g[l].astype(F32),
                                        w_router[l].astype(F32), b_router[l].astype(F32), tm)

        gates, slots, items = _route(logits, n_experts, tm_moe, tm_tok)
        w1g, w1u = deinterleave_gate_up(w_mlp1[l], _pick_block(d, (512, 256, 128)))
        x_sorted = moe_dispatch(h2, slots, tm_tok)
        rows = moe_ffn(x_sorted, items, w1g, w1u,
                       b_mlp1[l][:, None, 0::2].astype(F32), b_mlp1[l][:, None, 1::2].astype(F32),
                       w_mlp2[l].astype(BF16), b_mlp2[l][:, None, :].astype(F32), tm_moe)
        x = moe_combine(rows, slots, gates, x1, tm_comb)

        hd = 2 * A_HD
        k_p.append(k_pr.reshape(batch, lp, heads, hd))
        v_p.append(v_pr.reshape(batch, lp, heads, hd))
        k_s.append(k_sm.reshape(dec_batch, dec_seq, heads, hd))
        v_s.append(v_sm.reshape(dec_batch, dec_seq, heads, hd))
        s_p.append(sp)
        s_s.append(ss)

    y_prompt = x[:n_p].reshape(batch, lp, d)[:, N_META:]
    y_sample = x[n_p:].reshape(dec_batch, dec_seq, d)
    return (y_prompt, y_sample, jnp.stack(k_p), jnp.stack(v_p), jnp.stack(s_p),
            jnp.stack(k_s), jnp.stack(v_s), jnp.stack(s_s))
```

```python
import functools
import math

import numpy as np
import jax
import jax.numpy as jnp
from jax import lax
from jax.experimental import pallas as pl
from jax.experimental.pallas import tpu as pltpu

F32 = jnp.float32
BF16 = jnp.bfloat16

CHUNK = 64
N_META = 16
A_HD = 64
TOP_K = 4
SWIGLU_ALPHA = 1.702
SWIGLU_LIMIT = 7.0
EPS = 1e-6
NEG_INF = -1e30
LOG2E = math.log2(math.e)
REC_SUB = 16

LANES = 128
SUBLANES = 8
ATT_BLOCK = 2 * CHUNK
ATT_KEY_COLS = 256
ATT_TABLE_COLS = 512
MXU_DIM = 256
PROJ_ROWS = 1040
VMEM_LIMIT = 48 * 1024 * 1024


def _largest_block(n, limit, multiple=16):
    for c in range(limit - limit % multiple, 0, -multiple):
        if n % c == 0:
            return c
    raise ValueError(f"no block of multiple {multiple} divides {n}")


def _pick_block(n, candidates):
    for c in candidates:
        if n % c == 0:
            return c
    raise ValueError(f"no block size in {candidates} divides {n}")


def _dot(a, b, **kw):
    return jnp.dot(a, b, preferred_element_type=F32, **kw)


def _dot_nt(a, b):
    return lax.dot_general(a, b, (((1,), (1,)), ((), ())), preferred_element_type=F32)


def _dot_tn(a, b):
    return lax.dot_general(a, b, (((0,), (0,)), ((), ())), preferred_element_type=F32)


def _store_row_tiles(ref, x):
    rows = x.shape[0]
    for s in range(SUBLANES):
        ref[pl.ds(s, rows, stride=SUBLANES), :] = x[:, s * LANES:(s + 1) * LANES]


def _load_row_tiles(ref, rows):
    return jnp.concatenate([ref[pl.ds(s, rows, stride=SUBLANES), :] for s in range(SUBLANES)], axis=1)


def _tile_copy(src, src_row, dst, dst_row, sem):
    return pltpu.make_async_copy(src.at[pl.ds(pl.multiple_of(src_row * SUBLANES, SUBLANES), SUBLANES)],
                                 dst.at[pl.ds(pl.multiple_of(dst_row * SUBLANES, SUBLANES), SUBLANES)], sem)


def _params(*sem):
    return pltpu.CompilerParams(dimension_semantics=sem, vmem_limit_bytes=VMEM_LIMIT)


def _rmsnorm_kernel(x_ref, g_ref, o_ref):
    x = x_ref[...]
    ms = jnp.mean(x * x, axis=-1, keepdims=True)
    o_ref[...] = (x * lax.rsqrt(ms + EPS) * g_ref[...]).astype(o_ref.dtype)


def rmsnorm_rows(x, g, tm):
    n, d = x.shape
    return pl.pallas_call(
        _rmsnorm_kernel,
        out_shape=jax.ShapeDtypeStruct((n, d), BF16),
        grid=(n // tm,),
        in_specs=[pl.BlockSpec((tm, d), lambda i: (i, 0)),
                  pl.BlockSpec((1, d), lambda i: (0, 0))],
        out_specs=pl.BlockSpec((tm, d), lambda i: (i, 0)),
        compiler_params=_params("parallel"),
        name="rmsnorm_rows",
    )(x, g.reshape(1, d))


def _proj_kernel(h_ref, w_ref, o_ref):
    o_ref[...] = _dot(h_ref[...], w_ref[...]).astype(o_ref.dtype)


def _proj_groupnorm_kernel(h_ref, w_ref, gsum_ref, gexp_ref, gain_ref, o_ref):
    z = _dot(h_ref[...], w_ref[...])
    ms = _dot((z * z).astype(BF16), gsum_ref[...])
    inv = lax.rsqrt(ms + EPS)
    inv_hi = inv.astype(BF16)
    inv_lo = (inv - inv_hi.astype(F32)).astype(BF16)
    inv_full = _dot(inv_hi, gexp_ref[...]) + _dot(inv_lo, gexp_ref[...])
    o_ref[...] = (z * inv_full * gain_ref[...]).astype(o_ref.dtype)


def project(h, w, out_dtype, tn, group_gain=None, group=A_HD, row0=0, n_rows=None):
    k = h.shape[1]
    n = h.shape[0] if n_rows is None else n_rows
    m = w.shape[1]
    tm = _largest_block(math.gcd(n, row0) if row0 else n, PROJ_ROWS)
    grid = (m // tn, n // tm)
    blk0 = row0 // tm
    h_spec = pl.BlockSpec((tm, k), lambda j, i: (blk0 + i, 0))
    w_spec = pl.BlockSpec((k, tn), lambda j, i: (0, j))
    o_spec = pl.BlockSpec((tm, tn), lambda j, i: (i, j))
    out_shape = jax.ShapeDtypeStruct((n, m), out_dtype)
    if group_gain is None:
        return pl.pallas_call(
            _proj_kernel, out_shape=out_shape, grid=grid,
            in_specs=[h_spec, w_spec], out_specs=o_spec,
            compiler_params=_params("parallel", "parallel"), name="project",
        )(h, w)
    ng = tn // group
    cols = np.arange(tn) // group
    gsum = jnp.asarray((cols[:, None] == np.arange(ng)[None, :]) / group, BF16)
    gexp = jnp.asarray(np.arange(ng)[:, None] == cols[None, :], BF16)
    return pl.pallas_call(
        _proj_groupnorm_kernel, out_shape=out_shape, grid=grid,
        in_specs=[h_spec, w_spec,
                  pl.BlockSpec((tn, ng), lambda j, i: (0, 0)),
                  pl.BlockSpec((ng, tn), lambda j, i: (0, 0)),
                  pl.BlockSpec((1, tn), lambda j, i: (0, j))],
        out_specs=o_spec,
        compiler_params=_params("parallel", "parallel"), name="project_groupnorm",
    )(h, w, gsum, gexp, group_gain.reshape(1, m).astype(F32))


def _map_masks(q):
    lane = lax.broadcasted_iota(jnp.int32, q.shape, 1)
    zero = jnp.zeros_like(q)
    return (jnp.where(lane < A_HD, q, zero), jnp.where(lane >= A_HD, q, zero))


def _subln(o, g_ref, post_scale):
    ms = jnp.mean(o * o, axis=-1, keepdims=True)
    return o * lax.rsqrt(ms + EPS) * g_ref[...] * post_scale


def _attn_prompt_kernel(lam_ref, slope_ref, q_ref, k_ref, v_ref, g_ref, o_ref, kb, vb, bias_tab, s_scr,
                        *, seq_len, post_scale):
    tb = ATT_BLOCK
    n_frames = seq_len - N_META
    n_blocks = n_frames // tb
    slope = slope_ref[pl.program_id(1)] * LOG2E
    lam = lam_ref[0]
    kb[...] = k_ref[...].astype(BF16)
    vb[...] = v_ref[...].astype(BF16)

    qm = _map_masks(q_ref[0:N_META, :])
    r = lax.broadcasted_iota(jnp.int32, (N_META, N_META), 0)
    c = lax.broadcasted_iota(jnp.int32, (N_META, N_META), 1)
    bias = -slope * jnp.abs(r - c).astype(F32)
    km = kb[0:N_META, :]
    vm = vb[0:N_META, :]
    ws = []
    for m in range(2):
        s = _dot_nt(qm[m], km) + bias
        p = jnp.exp2(s - jnp.max(s, axis=-1, keepdims=True))
        ws.append(p / jnp.sum(p, axis=-1, keepdims=True))
    w = ws[0] - lam * ws[1]
    o_ref[0:N_META, :] = _subln(_dot(w.astype(BF16), vm), g_ref, post_scale).astype(o_ref.dtype)

    for c0 in range(0, n_frames, ATT_TABLE_COLS):
        wd = min(ATT_TABLE_COLS, n_frames - c0)
        r = lax.broadcasted_iota(jnp.int32, (tb, wd), 0)
        u = lax.broadcasted_iota(jnp.int32, (tb, wd), 1) + (c0 - (n_frames - tb))
        hidden = u >= (r // CHUNK + 1) * CHUNK
        bias_tab[:, c0:c0 + wd] = jnp.where(hidden, NEG_INF, -slope * jnp.abs(r - u).astype(F32))
    rm = lax.broadcasted_iota(jnp.int32, (tb, N_META), 0)
    cm = lax.broadcasted_iota(jnp.int32, (tb, N_META), 1)
    meta_bias0 = -slope * (N_META + rm - cm).astype(F32)

    def lane_tiles(x):
        return [x[:, t * LANES:(t + 1) * LANES] for t in range(x.shape[1] // LANES)]

    def key_chunks(i):
        n_keys = tb * (i + 1)
        return [(c0, min(ATT_KEY_COLS, n_keys - c0)) for c0 in range(0, n_keys, ATT_KEY_COLS)]

    def score_pass(i, m, qm):
        off = n_frames - tb * (i + 1)
        scr = s_scr.at[i % 2, m]
        s_meta = _dot_nt(qm[m], km) + (meta_bias0 - slope * float(tb * i))
        mx = None
        for c0, wd in key_chunks(i):
            s = (_dot_nt(qm[m], kb[N_META + c0:N_META + c0 + wd, :])
                 + bias_tab[:, off + c0:off + c0 + wd])
            scr[:, c0:c0 + wd] = s
            for t in lane_tiles(s):
                mx = t if mx is None else jnp.maximum(mx, t)
        m_row = jnp.maximum(jnp.max(mx, axis=-1, keepdims=True), jnp.max(s_meta, axis=-1, keepdims=True))
        return s_meta, m_row

    def value_pass(i, m, s_meta, m_row):
        scr = s_scr.at[i % 2, m]
        p_meta = jnp.exp2(s_meta - m_row)
        acc = _dot(p_meta.astype(BF16), vm)
        lsum = None
        for c0, wd in key_chunks(i):
            p = jnp.exp2(scr[:, c0:c0 + wd] - m_row)
            for t in lane_tiles(p):
                lsum = t if lsum is None else lsum + t
            acc = acc + _dot(p.astype(BF16), vb[N_META + c0:N_META + c0 + wd, :])
        l_row = jnp.sum(lsum, axis=-1, keepdims=True) + jnp.sum(p_meta, axis=-1, keepdims=True)
        return acc / l_row

    units = [(i, m) for i in range(n_blocks) for m in range(2)]
    qms, outs = {}, {}

    def finish(i, m, s_meta, m_row):
        outs[m] = value_pass(i, m, s_meta, m_row)
        if m == 1:
            q0 = N_META + i * tb
            o = outs[0] - lam * outs[1]
            o_ref[q0:q0 + tb, :] = _subln(o, g_ref, post_scale).astype(o_ref.dtype)

    pending = None
    for i, m in units:
        if m == 0:
            q0 = N_META + i * tb
            qms[i] = _map_masks(q_ref[q0:q0 + tb, :])
        scored = score_pass(i, m, qms[i])
        if pending is not None:
            finish(*pending)
        pending = (i, m) + scored
    finish(*pending)


def attention_prompt(q, k, v, lam, slopes, subln_g, batch, seq_len, heads, row_block0, post_scale):
    hw = 2 * A_HD
    spec = lambda: pl.BlockSpec((seq_len, hw), lambda b, h, *_: (row_block0 + b, h))
    kern = functools.partial(_attn_prompt_kernel, seq_len=seq_len, post_scale=post_scale)
    return pl.pallas_call(
        kern,
        out_shape=jax.ShapeDtypeStruct((batch * seq_len, heads * hw), BF16),
        grid_spec=pltpu.PrefetchScalarGridSpec(
            num_scalar_prefetch=2, grid=(batch, heads),
            in_specs=[spec(), spec(), spec(), pl.BlockSpec((1, hw), lambda b, h, *_: (0, 0))],
            out_specs=pl.BlockSpec((seq_len, hw), lambda b, h, *_: (b, h)),
            scratch_shapes=[pltpu.VMEM((seq_len, hw), BF16), pltpu.VMEM((seq_len, hw), BF16),
                            pltpu.VMEM((ATT_BLOCK, seq_len - N_META), F32),
                            pltpu.VMEM((2, 2, ATT_BLOCK, seq_len - N_META), F32)]),
        compiler_params=_params("parallel", "parallel"), name="attention_prompt",
    )(lam, slopes, q, k, v, subln_g.reshape(1, hw))


def _attn_sample_kernel(lam_ref, slope_ref, q_ref, k_ref, v_ref, ck_ref, cv_ref, g_ref, o_ref,
                        *, post_scale):
    slope = slope_ref[pl.program_id(1)] * LOG2E
    lam = lam_ref[0]
    t = q_ref.shape[0]
    past = ck_ref.shape[1]
    qm = _map_masks(q_ref[...])
    kn = k_ref[...].astype(BF16)
    vn = v_ref[...].astype(BF16)
    kc = ck_ref[0].astype(BF16)
    vc = cv_ref[0].astype(BF16)
    r = lax.broadcasted_iota(jnp.int32, (t, past), 0)
    c = lax.broadcasted_iota(jnp.int32, (t, past), 1)
    bias_c = -slope * (past + r - c).astype(F32)
    r = lax.broadcasted_iota(jnp.int32, (t, t), 0)
    c = lax.broadcasted_iota(jnp.int32, (t, t), 1)
    bias_n = -slope * jnp.abs(r - c).astype(F32)
    wc, wn = [], []
    for m in range(2):
        sc = _dot_nt(qm[m], kc) + bias_c
        sn = _dot_nt(qm[m], kn) + bias_n
        mx = jnp.maximum(jnp.max(sc, axis=-1, keepdims=True), jnp.max(sn, axis=-1, keepdims=True))
        pc = jnp.exp2(sc - mx)
        pn = jnp.exp2(sn - mx)
        den = jnp.sum(pc, axis=-1, keepdims=True) + jnp.sum(pn, axis=-1, keepdims=True)
        wc.append(pc / den)
        wn.append(pn / den)
    o = (_dot((wc[0] - lam * wc[1]).astype(BF16), vc)
         + _dot((wn[0] - lam * wn[1]).astype(BF16), vn))
    o_ref[...] = _subln(o, g_ref, post_scale).astype(o_ref.dtype)


def attention_sample(q, k, v, cache_k, cache_v, cache_block0, lam, slopes, subln_g, batch, t, heads,
                     row_block0, post_scale):
    hw = 2 * A_HD
    past = cache_k.shape[1]
    new_q = pl.BlockSpec((t, hw), lambda b, h, *_: (row_block0 + b, h))
    new = lambda: pl.BlockSpec((t, hw), lambda b, h, *_: (b, h))
    cache = lambda: pl.BlockSpec((1, past, hw), lambda b, h, *_: (cache_block0 + b, 0, h))
    kern = functools.partial(_attn_sample_kernel, post_scale=post_scale)
    return pl.pallas_call(
        kern,
        out_shape=jax.ShapeDtypeStruct((batch * t, heads * hw), BF16),
        grid_spec=pltpu.PrefetchScalarGridSpec(
            num_scalar_prefetch=2, grid=(batch, heads),
            in_specs=[new_q, new(), new(), cache(), cache(),
                      pl.BlockSpec((1, hw), lambda b, h, *_: (0, 0))],
            out_specs=pl.BlockSpec((t, hw), lambda b, h, *_: (b, h))),
        compiler_params=_params("parallel", "parallel"), name="attention_sample",
    )(lam, slopes, q, k, v, cache_k, cache_v, subln_g.reshape(1, hw))


def _hgrn_prepare(start, size, heads_per_step, refs, tri):
    q_ref, f_ref, i_ref, g_ref, lb_ref, ng_ref, o_ref = refs
    nsub = size // REC_SUB
    rows = pl.ds(start, size)
    width = heads_per_step * LANES
    row = lax.broadcasted_iota(jnp.int32, (size, width), 0)
    lb = lb_ref[...]
    fg = lb + (1.0 - lb) * jax.nn.sigmoid(f_ref[rows, :])
    kk = 1.0 - fg
    logf = jnp.log(fg)
    qq = jax.nn.silu(q_ref[rows, :].astype(F32))
    vv = i_ref[rows, :].astype(BF16)
    lf_hi = logf.astype(BF16)
    lf_lo = (logf - lf_hi.astype(F32)).astype(BF16)
    b = _dot(tri, lf_hi) + _dot(tri, lf_lo)
    r_parts = [jnp.zeros((REC_SUB, width), F32)]
    for s in range(1, nsub):
        r_parts.append(jnp.broadcast_to(b[s * REC_SUB - 1:s * REC_SUB, :], (REC_SUB, width)))
    r_sub = r_parts[0] if nsub == 1 else jnp.concatenate(r_parts, axis=0)
    q_sub = qq * jnp.exp(b - r_sub)
    q_dec = (q_sub * jnp.exp(r_sub)).astype(BF16)
    b_last = b[size - 1:size, :]
    k_end = (kk * jnp.exp(b_last - b)).astype(BF16)
    q_parts, k_parts = [], []
    for i in range(nsub):
        in_sub = (row >= i * REC_SUB) & (row < (i + 1) * REC_SUB)
        q_parts.append(jnp.where(in_sub, q_sub, 0.0).astype(BF16))
        rel = jnp.where(row < (i + 1) * REC_SUB, r_sub[i * REC_SUB:i * REC_SUB + 1, :] - b, NEG_INF)
        k_parts.append((kk * jnp.exp(rel)).astype(BF16))
    return dict(rows=rows, size=size, q_dec=q_dec, q_parts=q_parts, k_parts=k_parts, k_end=k_end,
                vv=vv, decay=jnp.exp(b_last))


def _hgrn_scores(prep, p):
    lanes = slice(p * LANES, (p + 1) * LANES)
    size = prep["size"]
    q_hat = jnp.concatenate([x[:, lanes] for x in prep["q_parts"]], axis=1)
    k_hat = jnp.concatenate([x[:, lanes] for x in prep["k_parts"]], axis=1)
    rr = lax.broadcasted_iota(jnp.int32, (size, size), 0)
    cc = lax.broadcasted_iota(jnp.int32, (size, size), 1)
    a = jnp.where(cc <= rr, _dot_nt(q_hat, k_hat), 0.0).astype(BF16)
    kv = _dot_tn(prep["k_end"][:, lanes], prep["vv"][:, lanes])
    return a, kv


def _hgrn_finish(prep, scores, p, refs, st_ref):
    q_ref, f_ref, i_ref, g_ref, lb_ref, ng_ref, o_ref = refs
    lanes = slice(p * LANES, (p + 1) * LANES)
    rows = prep["rows"]
    a, kv = scores
    st = st_ref[p]
    o = _dot(jnp.concatenate([prep["q_dec"][:, lanes], a], axis=1),
             jnp.concatenate([st.astype(BF16), prep["vv"][:, lanes]], axis=0))
    ms = jnp.mean(o * o, axis=-1, keepdims=True)
    on = o * lax.rsqrt(ms + EPS) * ng_ref[...]
    gg = g_ref[rows, lanes].astype(F32)
    o_ref[rows, lanes] = (on * jax.nn.silu(gg)).astype(o_ref.dtype)
    decay_cols = jnp.broadcast_to(prep["decay"][:, lanes], (LANES, LANES)).T
    st_ref[p] = st * decay_cols + kv


def _hgrn_chunks(starts, size, heads_per_step, refs, st_ref, tri):
    preps = [_hgrn_prepare(s, size, heads_per_step, refs, tri) for s in starts]
    scores = [[_hgrn_scores(prep, p) for p in range(heads_per_step)] for prep in preps]
    for prep, sc in zip(preps, scores):
        for p in range(heads_per_step):
            _hgrn_finish(prep, sc[p], p, refs, st_ref)


def _hgrn_consts(size):
    r = lax.broadcasted_iota(jnp.int32, (size, size), 0)
    c = lax.broadcasted_iota(jnp.int32, (size, size), 1)
    return jnp.where(c <= r, 1.0, 0.0).astype(BF16)


def _hgrn_kernel(*args, seq_len, heads_per_step, has_state):
    if has_state:
        q_ref, f_ref, i_ref, g_ref, lb_ref, ng_ref, s0_ref, o_ref, s_ref, st_ref = args
    else:
        q_ref, f_ref, i_ref, g_ref, lb_ref, ng_ref, o_ref, s_ref, st_ref = args
    refs = (q_ref, f_ref, i_ref, g_ref, lb_ref, ng_ref, o_ref)
    for p in range(heads_per_step):
        if has_state:
            st_ref[p] = s0_ref[0, p]
        else:
            st_ref[p] = jnp.zeros(st_ref.shape[1:], F32)
    _hgrn_chunks([0], N_META, heads_per_step, refs, st_ref, _hgrn_consts(N_META))
    n_chunks = (seq_len - N_META) // CHUNK
    if n_chunks:
        tri = _hgrn_consts(CHUNK)
        per_step = 2 if n_chunks % 2 == 0 else 1

        def body(ci, _):
            starts = [pl.multiple_of(N_META + (ci * per_step + j) * CHUNK, N_META) for j in range(per_step)]
            _hgrn_chunks(starts, CHUNK, heads_per_step, refs, st_ref, tri)
            return 0

        lax.fori_loop(0, n_chunks // per_step, body, 0)
    for p in range(heads_per_step):
        s_ref[0, p] = st_ref[p]


def hgrn2(zr, f_b, lb, norm_g, s0, batch, seq_len, heads, row_block0, heads_per_step):
    dk = LANES
    hpw = heads_per_step * dk
    d = heads * dk
    nhp = heads // heads_per_step
    col = lambda seg: pl.BlockSpec((seq_len, hpw), lambda b, hp, seg=seg: (row_block0 + b, seg * nhp + hp))
    in_specs = [col(0), pl.BlockSpec((seq_len, hpw), lambda b, hp: (row_block0 + b, hp)), col(1), col(2),
                pl.BlockSpec((1, hpw), lambda b, hp: (0, hp)),
                pl.BlockSpec((1, dk), lambda b, hp: (0, 0))]
    args = [zr, f_b, zr, zr, lb.reshape(1, d), norm_g.reshape(1, dk)]
    state_spec = pl.BlockSpec((1, heads_per_step, dk, dk), lambda b, hp: (b, hp, 0, 0))
    if s0 is not None:
        in_specs.append(state_spec)
        args.append(s0)
    kern = functools.partial(_hgrn_kernel, seq_len=seq_len, heads_per_step=heads_per_step,
                             has_state=s0 is not None)
    return pl.pallas_call(
        kern,
        out_shape=(jax.ShapeDtypeStruct((batch * seq_len, d), BF16),
                   jax.ShapeDtypeStruct((batch, heads, dk, dk), F32)),
        grid=(batch, nhp),
        in_specs=in_specs,
        out_specs=(pl.BlockSpec((seq_len, hpw), lambda b, hp: (b, hp)), state_spec),
        scratch_shapes=[pltpu.VMEM((heads_per_step, dk, dk), F32)],
        compiler_params=_params("parallel", "parallel"), name="hgrn2",
    )(*args)


def _merge_kernel(oa_ref, ob_ref, ga_ref, gb_ref, x_ref, wa_ref, wb_ref, wo_ref, g2_ref, wr_ref, br_ref,
                  x1_ref, h2_ref, lg_ref):
    ya = _dot(oa_ref[...], wa_ref[...])
    yb = _dot(ob_ref[...], wb_ref[...])
    mix = jax.nn.sigmoid(ga_ref[...].astype(F32)) * ya + jax.nn.sigmoid(gb_ref[...].astype(F32)) * yb
    x1 = x_ref[...] + _dot(mix.astype(BF16), wo_ref[...])
    x1_ref[...] = x1
    ms = jnp.mean(x1 * x1, axis=-1, keepdims=True)
    h2 = x1 * lax.rsqrt(ms + EPS) * g2_ref[...]
    _store_row_tiles(h2_ref, h2)
    ne = lg_ref.shape[1]
    h_hi = h2.astype(BF16)
    h_lo = (h2 - h_hi.astype(F32)).astype(BF16)
    both = _dot(h_hi, wr_ref[...])
    lg_ref[...] = both[:, :ne] + both[:, ne:] + _dot(h_lo, wr_ref[:, :ne]) + br_ref[...]


def merge_branches(oa, ob, zr, x, wa, wb, wo, g2, w_router, b_router, tm):
    n, d = x.shape
    ne = w_router.shape[1]
    row = lambda: pl.BlockSpec((tm, d), lambda i: (i, 0))
    full = lambda a: pl.BlockSpec(a.shape, lambda i: (0, 0))
    g2 = g2.reshape(1, d)
    b_router = b_router.reshape(1, ne)
    w_hi = w_router.astype(BF16)
    w_router = jnp.concatenate([w_hi, (w_router - w_hi.astype(F32)).astype(BF16)], axis=1)
    return pl.pallas_call(
        _merge_kernel,
        out_shape=(jax.ShapeDtypeStruct((n, d), F32), jax.ShapeDtypeStruct((n * SUBLANES, LANES), F32),
                   jax.ShapeDtypeStruct((n, ne), F32)),
        grid=(n // tm,),
        in_specs=[row(), row(),
                  pl.BlockSpec((tm, d), lambda i: (i, 3)), pl.BlockSpec((tm, d), lambda i: (i, 4)),
                  row(), full(wa), full(wb), full(wo), full(g2), full(w_router), full(b_router)],
        out_specs=(row(), pl.BlockSpec((tm * SUBLANES, LANES), lambda i: (i, 0)),
                   pl.BlockSpec((tm, ne), lambda i: (i, 0))),
        compiler_params=_params("parallel"), name="merge_branches",
    )(oa, ob, zr, zr, x, wa, wb, wo, g2, w_router, b_router)


def _deinterleave_kernel(w_ref, perm_ref, g_ref, u_ref):
    perm = perm_ref[...]
    half = MXU_DIM // 2
    for t in range(w_ref.shape[2] // MXU_DIM):
        y = _dot(w_ref[0, :, t * MXU_DIM:(t + 1) * MXU_DIM].astype(BF16), perm)
        g_ref[0, :, t * half:(t + 1) * half] = y[:, :half].astype(BF16)
        u_ref[0, :, t * half:(t + 1) * half] = y[:, half:].astype(BF16)


def deinterleave_gate_up(w1, rows):
    e, k, n2 = w1.shape
    half = MXU_DIM // 2
    idx = np.arange(MXU_DIM)
    perm = np.zeros((MXU_DIM, MXU_DIM), np.float32)
    perm[idx, np.where(idx % 2 == 0, idx // 2, half + idx // 2)] = 1.0
    out = jax.ShapeDtypeStruct((e, k, n2 // 2), BF16)
    out_spec = lambda: pl.BlockSpec((1, rows, n2 // 2), lambda i, j: (i, j, 0))
    return pl.pallas_call(
        _deinterleave_kernel, out_shape=(out, out), grid=(e, k // rows),
        in_specs=[pl.BlockSpec((1, rows, n2), lambda i, j: (i, j, 0)),
                  pl.BlockSpec((MXU_DIM, MXU_DIM), lambda i, j: (0, 0))],
        out_specs=(out_spec(), out_spec()),
        compiler_params=_params("parallel", "parallel"), name="deinterleave_gate_up",
    )(w1, jnp.asarray(perm, BF16))


DISPATCH_RING = 3


def _moe_dispatch_kernel(slot_ref, h_hbm, o_hbm, stage, sem_in, sem_out):
    i = pl.program_id(0)
    nb = pl.num_programs(0)
    tm = slot_ref.shape[2] // TOP_K
    rows = tm * SUBLANES

    def load(step):
        s = lax.rem(step, DISPATCH_RING)
        src = h_hbm.at[pl.ds(pl.multiple_of(step * rows, rows), rows)]
        return pltpu.make_async_copy(src, stage.at[s], sem_in.at[s])

    def wait_scatter(step):
        s = lax.rem(step, DISPATCH_RING)

        def body(r, _):
            for k in range(TOP_K):
                _tile_copy(stage.at[s], r, o_hbm, 0, sem_out.at[s]).wait()
            return 0
        lax.fori_loop(0, tm, body, 0, unroll=4)

    @pl.when(i == 0)
    def _():
        load(i).start()

    @pl.when(i >= DISPATCH_RING - 1)
    def _():
        wait_scatter(i - (DISPATCH_RING - 1))

    @pl.when(i + 1 < nb)
    def _():
        load(i + 1).start()

    load(i).wait()
    cur = lax.rem(i, DISPATCH_RING)

    def body(r, _):
        for k in range(TOP_K):
            _tile_copy(stage.at[cur], r, o_hbm, slot_ref[0, 0, k * tm + r],
                       sem_out.at[cur]).start(priority=k % 2)
        return 0
    lax.fori_loop(0, tm, body, 0, unroll=2)

    @pl.when(i == nb - 1)
    def _():
        for back in range(DISPATCH_RING - 2, -1, -1):
            @pl.when(i - back >= 0)
            def _(back=back):
                wait_scatter(i - back)


def moe_dispatch(h2_tiles, slots, tm):
    n = slots.shape[0]
    nb = n // tm
    ids_arr = slots.reshape(nb, tm, TOP_K).transpose(0, 2, 1).reshape(nb, 1, TOP_K * tm)
    return pl.pallas_call(
        _moe_dispatch_kernel,
        out_shape=jax.ShapeDtypeStruct((n * TOP_K * SUBLANES, LANES), F32),
        grid=(nb,),
        in_specs=[pl.BlockSpec((1, 1, TOP_K * tm), lambda i: (i, 0, 0), memory_space=pltpu.SMEM),
                  pl.BlockSpec(memory_space=pl.ANY)],
        out_specs=pl.BlockSpec(memory_space=pl.ANY),
        scratch_shapes=[pltpu.VMEM((DISPATCH_RING, tm * SUBLANES, LANES), F32),
                        pltpu.SemaphoreType.DMA((DISPATCH_RING,)),
                        pltpu.SemaphoreType.DMA((DISPATCH_RING,))],
        compiler_params=_params("arbitrary"), name="moe_dispatch",
    )(ids_arr, h2_tiles)


def _moe_ffn_kernel(blk_ref, exp_ref, lo_ref, hi_ref, first_ref, x_ref, w1g_ref, w1u_ref, b1g_ref,
                    b1u_ref, w2_ref, b2_ref, o_ref):
    i = pl.program_id(0)
    tm = x_ref.shape[0] // SUBLANES
    lo = lo_ref[i]
    hi = hi_ref[i]

    @pl.when(hi > lo)
    def _():
        x = _load_row_tiles(x_ref, tm).astype(BF16)
        gate = jnp.minimum(_dot(x, w1g_ref[0]) + b1g_ref[0], SWIGLU_LIMIT)
        up = jnp.clip(_dot(x, w1u_ref[0]) + b1u_ref[0], -SWIGLU_LIMIT, SWIGLU_LIMIT)
        act = gate * jax.nn.sigmoid(SWIGLU_ALPHA * gate) * (up + 1.0)
        y = _dot(act.astype(BF16), w2_ref[0]) + b2_ref[0]
        row = lax.broadcasted_iota(jnp.int32, (tm, LANES), 0)
        mine = (row >= lo) & (row < hi)

        def write(keep_old):
            for s in range(SUBLANES):
                dst = (pl.ds(s, tm, stride=SUBLANES), slice(None))
                old = o_ref[dst] if keep_old else 0.0
                o_ref[dst] = jnp.where(mine, y[:, s * LANES:(s + 1) * LANES], old)

        pl.when(first_ref[i] == 1)(functools.partial(write, False))
        pl.when(first_ref[i] == 0)(functools.partial(write, True))


def moe_ffn(x_tiles, items, w1g, w1u, b1g, b1u, w2, b2, tm):
    n_items = items[0].shape[0]
    wspec = lambda a: pl.BlockSpec((1,) + a.shape[1:], lambda i, blk, exp, *_: (exp[i], 0, 0))
    rows = lambda: pl.BlockSpec((tm * SUBLANES, LANES), lambda i, blk, *_: (blk[i], 0))
    return pl.pallas_call(
        _moe_ffn_kernel,
        out_shape=jax.ShapeDtypeStruct(x_tiles.shape, F32),
        grid_spec=pltpu.PrefetchScalarGridSpec(
            num_scalar_prefetch=5, grid=(n_items,),
            in_specs=[rows(), wspec(w1g), wspec(w1u), wspec(b1g), wspec(b1u), wspec(w2), wspec(b2)],
            out_specs=rows()),
        compiler_params=_params("arbitrary"), name="moe_ffn",
    )(*items, x_tiles, w1g, w1u, b1g, b1u, w2, b2)


def _moe_combine_kernel(slot_ref, slot_next_ref, gate_ref, x_ref, rows_hbm, o_ref, buf, sem):
    i = pl.program_id(0)
    nb = pl.num_programs(0)
    tm = buf.shape[2] // SUBLANES
    slot = lax.rem(i, 2)

    def start_rows(ids_ref, s):
        def body(r, _):
            for k in range(TOP_K):
                _tile_copy(rows_hbm, ids_ref[0, 0, k * tm + r], buf.at[s, k], r,
                           sem.at[s]).start(priority=k % 2)
            return 0
        lax.fori_loop(0, tm, body, 0, unroll=2)

    @pl.when(i == 0)
    def _():
        start_rows(slot_ref, 0)

    @pl.when(i + 1 < nb)
    def _():
        start_rows(slot_next_ref, 1 - slot)

    def wait_body(r, _):
        for k in range(TOP_K):
            _tile_copy(rows_hbm, 0, buf.at[slot, k], r, sem.at[slot]).wait()
        return 0
    lax.fori_loop(0, tm, wait_body, 0, unroll=4)

    g = gate_ref[...]
    gk = [g[:, k:k + 1] for k in range(TOP_K)]
    for s in range(SUBLANES):
        piece = pl.ds(s, tm, stride=SUBLANES)
        y = buf[slot, 0, piece, :] * gk[0]
        for k in range(1, TOP_K):
            y = y + buf[slot, k, piece, :] * gk[k]
        o_ref[:, s * LANES:(s + 1) * LANES] = x_ref[:, s * LANES:(s + 1) * LANES] + y


def moe_combine(rows, slots, gates, x1, tm):
    n, d = x1.shape
    nb = n // tm
    ids_arr = slots.reshape(nb, tm, TOP_K).transpose(0, 2, 1).reshape(nb, 1, TOP_K * tm)
    ids = lambda f: pl.BlockSpec((1, 1, TOP_K * tm), f, memory_space=pltpu.SMEM)
    return pl.pallas_call(
        _moe_combine_kernel,
        out_shape=jax.ShapeDtypeStruct((n, d), F32),
        grid=(nb,),
        in_specs=[ids(lambda i: (i, 0, 0)),
                  ids(lambda i: (jnp.minimum(i + 1, nb - 1), 0, 0)),
                  pl.BlockSpec((tm, TOP_K), lambda i: (i, 0)),
                  pl.BlockSpec((tm, d), lambda i: (i, 0)),
                  pl.BlockSpec(memory_space=pl.ANY)],
        out_specs=pl.BlockSpec((tm, d), lambda i: (i, 0)),
        scratch_shapes=[pltpu.VMEM((2, TOP_K, tm * SUBLANES, LANES), F32), pltpu.SemaphoreType.DMA((2,))],
        compiler_params=_params("arbitrary"), name="moe_combine",
    )(ids_arr, ids_arr, gates, x1, rows)


def _route(logits, n_experts, tm, tb):
    n = logits.shape[0]
    nk = n * TOP_K
    i32 = jnp.int32
    top_val, top_idx = lax.top_k(logits, TOP_K)
    gates = jax.nn.softmax(top_val, axis=-1)
    eid = top_idx.astype(i32)
    chosen = (eid[:, :, None] == jnp.arange(n_experts, dtype=i32)[None, None, :]).any(axis=1)
    nb = n // tb
    cb = chosen.reshape(nb, tb, n_experts).astype(BF16)
    before = jnp.asarray(np.tril(np.ones((tb, tb), np.float32), -1), BF16)
    within = jnp.einsum('ts,bse->bte', before, cb, preferred_element_type=F32)
    per_block = jnp.sum(cb.astype(F32), axis=1)
    block_off = jnp.cumsum(per_block, axis=0) - per_block
    rank = (within + block_off[:, None, :]).reshape(n, n_experts)
    counts = jnp.sum(per_block, axis=0).astype(i32)
    gend = jnp.cumsum(counts)
    gstart = gend - counts
    slots = gstart[eid] + jnp.take_along_axis(rank, eid, axis=1).astype(i32)

    n_blk = nk // tm
    n_items = n_blk + n_experts - 1
    first_blk = gstart // tm
    tiles = jnp.where(counts > 0, (gend - 1) // tm - first_blk + 1, 0)
    tile_end = jnp.cumsum(tiles)
    idx = jnp.arange(n_items, dtype=i32)
    valid = idx < tile_end[-1]
    exp = jnp.minimum(jnp.sum((tile_end[None, :] <= idx[:, None]).astype(i32), axis=1), n_experts - 1)
    blk = jnp.where(valid, first_blk[exp] + idx - (tile_end[exp] - tiles[exp]), n_blk - 1)
    lo = jnp.where(valid, jnp.clip(gstart[exp] - blk * tm, 0, tm), 0)
    hi = jnp.where(valid, jnp.clip(gend[exp] - blk * tm, 0, tm), 0)
    first = (blk != jnp.concatenate([jnp.full((1,), -1, i32), blk[:-1]])).astype(i32)
    return gates, slots, (blk.astype(i32), exp, lo.astype(i32), hi.astype(i32), first)


def kernel(x_prompt, x_sample, cache_k, cache_v, state_hgrn, meta_tokens, norm_mix_g, w_in, q_norm_g,
           k_norm_g, lambda_q1, lambda_k1, lambda_q2, lambda_k2, subln_g, hgrn_lb, hgrn_norm_g, w_a, w_b,
           w_o, norm_ffn_g, w_router, b_router, w_mlp1, b_mlp1, w_mlp2, b_mlp2):
    batch, seq, d = x_prompt.shape
    dec_batch, dec_seq, _ = x_sample.shape
    depth = w_in.shape[0]
    heads = cache_k.shape[3]
    b_heads = state_hgrn.shape[2]
    n_experts = w_router.shape[-1]
    d_ff = w_mlp2.shape[2]
    lp = N_META + seq
    n_p = batch * lp
    n_s = dec_batch * dec_seq
    n = n_p + n_s
    assert dec_seq == N_META and seq % ATT_BLOCK == 0 and n_p % dec_seq == 0
    assert d == heads * 2 * A_HD == b_heads * LANES == SUBLANES * LANES

    meta = jnp.broadcast_to(meta_tokens[None].astype(x_prompt.dtype), (batch, N_META, d))
    x = jnp.concatenate([jnp.concatenate([meta, x_prompt], axis=1).reshape(n_p, d),
                         x_sample.reshape(n_s, d)], axis=0)

    tm = _pick_block(n, (512, 256, 128, 64, 32, 16))
    tm_moe = _pick_block(n * TOP_K, (256, 128, 64))
    tm_tok = _pick_block(n, (256, 128, 64, 32, 16))
    tm_comb = _pick_block(n, (128, 64, 32, 16))
    slopes = jnp.asarray(np.power(2.0, -8.0 * np.arange(1, heads + 1) / heads).astype(np.float32))
    lower_bounds = jnp.cumsum(jax.nn.softmax(hgrn_lb.astype(F32), axis=0), axis=0)

    k_p, v_p, s_p, k_s, v_s, s_s = [], [], [], [], [], []
    for l in range(depth):
        lam_init = 0.8 - 0.6 * math.exp(-0.3 * l)
        lam = (jnp.exp(jnp.sum(lambda_q1[l].astype(F32) * lambda_k1[l].astype(F32)))
               - jnp.exp(jnp.sum(lambda_q2[l].astype(F32) * lambda_k2[l].astype(F32))) + lam_init).reshape(1)
        w = w_in[l].astype(BF16)
        seg = lambda j: w[:, j * d:(j + 1) * d]

        h = rmsnorm_rows(x, norm_mix_g[l], tm)
        q_a = project(h, seg(0), BF16, d,
                      group_gain=jnp.tile(q_norm_g[l].astype(F32), d // A_HD) * (A_HD ** -0.5 * LOG2E))
        k_gain = jnp.tile(k_norm_g[l].astype(F32), d // A_HD)
        k_pr = project(h, seg(1), F32, d, group_gain=k_gain, n_rows=n_p)
        k_sm = project(h, seg(1), F32, d, group_gain=k_gain, row0=n_p, n_rows=n_s)
        v_pr = project(h, seg(2), F32, d, n_rows=n_p)
        v_sm = project(h, seg(2), F32, d, row0=n_p, n_rows=n_s)
        f_b = project(h, seg(4), F32, d)
        zr = project(h, jnp.concatenate([seg(3), seg(5), seg(6), seg(7), seg(8)], axis=1), BF16, d)

        post = 1.0 - lam_init
        oa_p = attention_prompt(q_a, k_pr, v_pr, lam, slopes, subln_g[l], batch, lp, heads, 0, post)
        oa_s = attention_sample(q_a, k_sm, v_sm, cache_k.reshape(depth * dec_batch, -1, d),
                                cache_v.reshape(depth * dec_batch, -1, d), l * dec_batch, lam, slopes,
                                subln_g[l], dec_batch, dec_seq, heads, n_p // dec_seq, post)
        oa = jnp.concatenate([oa_p, oa_s], axis=0)

        ob_p, sp = hgrn2(zr, f_b, lower_bounds[l], hgrn_norm_g[l], None, batch, lp, b_heads, 0, 4)
        ob_s, ss = hgrn2(zr, f_b, lower_bounds[l], hgrn_norm_g[l], state_hgrn[l], dec_batch, dec_seq,
                         b_heads, n_p // dec_seq, 4)
        ob = jnp.concatenate([ob_p, ob_s], axis=0)

        x1, h2, logits = merge_branches(oa, ob, zr, x, w_a[l].astype(BF16), w_b[l].astype(BF16),
                                        w_o[l].astype(BF16), norm_ffn_g[l].astype(F32),
                                        w_router[l].astype(F32), b_router[l].astype(F32), tm)

        gates, slots, items = _route(logits, n_experts, tm_moe, tm_tok)
        w1g, w1u = deinterleave_gate_up(w_mlp1[l], _pick_block(d, (512, 256, 128)))
        x_sorted = moe_dispatch(h2, slots, tm_tok)
        rows = moe_ffn(x_sorted, items, w1g, w1u,
                       b_mlp1[l][:, None, 0::2].astype(F32), b_mlp1[l][:, None, 1::2].astype(F32),
                       w_mlp2[l].astype(BF16), b_mlp2[l][:, None, :].astype(F32), tm_moe)
        x = moe_combine(rows, slots, gates, x1, tm_comb)

        hd = 2 * A_HD
        k_p.append(k_pr.reshape(batch, lp, heads, hd))
        v_p.append(v_pr.reshape(batch, lp, heads, hd))
        k_s.append(k_sm.reshape(dec_batch, dec_seq, heads, hd))
        v_s.append(v_sm.reshape(dec_batch, dec_seq, heads, hd))
        s_p.append(sp)
        s_s.append(ss)

    y_prompt = x[:n_p].reshape(batch, lp, d)[:, N_META:]
    y_sample = x[n_p:].reshape(dec_batch, dec_seq, d)
    return (y_prompt, y_sample, jnp.stack(k_p), jnp.stack(v_p), jnp.stack(s_p),
            jnp.stack(k_s), jnp.stack(v_s), jnp.stack(s_s))
```

```python
import functools
import math

import numpy as np
import jax
import jax.numpy as jnp
from jax import lax
from jax.experimental import pallas as pl
from jax.experimental.pallas import tpu as pltpu

F32 = jnp.float32
BF16 = jnp.bfloat16

CHUNK = 64
N_META = 16
A_HD = 64
TOP_K = 4
SWIGLU_ALPHA = 1.702
SWIGLU_LIMIT = 7.0
EPS = 1e-6
NEG_INF = -1e30
LOG2E = math.log2(math.e)
REC_SUB = 16

LANES = 128
SUBLANES = 8
ATT_BLOCK = 2 * CHUNK
ATT_KEY_COLS = 256
ATT_TABLE_COLS = 512
MXU_DIM = 256
PROJ_ROWS = 1040
MERGE_ROWS = 512
VMEM_LIMIT = 48 * 1024 * 1024
SAMPLE_VMEM_LIMIT = 56 * 1024 * 1024


def _largest_block(n, limit, multiple=16):
    for c in range(limit - limit % multiple, 0, -multiple):
        if n % c == 0:
            return c
    raise ValueError(f"no block of multiple {multiple} divides {n}")


def _pick_block(n, candidates):
    for c in candidates:
        if n % c == 0:
            return c
    raise ValueError(f"no block size in {candidates} divides {n}")


def _dot(a, b, **kw):
    return jnp.dot(a, b, preferred_element_type=F32, **kw)


def _dot_nt(a, b):
    return lax.dot_general(a, b, (((1,), (1,)), ((), ())), preferred_element_type=F32)


def _dot_tn(a, b):
    return lax.dot_general(a, b, (((0,), (0,)), ((), ())), preferred_element_type=F32)


def _store_row_tiles(ref, x):
    rows = x.shape[0]
    for s in range(SUBLANES):
        ref[pl.ds(s, rows, stride=SUBLANES), :] = x[:, s * LANES:(s + 1) * LANES]


def _load_row_tiles(ref, rows):
    return jnp.concatenate([ref[pl.ds(s, rows, stride=SUBLANES), :] for s in range(SUBLANES)], axis=1)


def _tile_copy(src, src_row, dst, dst_row, sem):
    return pltpu.make_async_copy(src.at[pl.ds(pl.multiple_of(src_row * SUBLANES, SUBLANES), SUBLANES)],
                                 dst.at[pl.ds(pl.multiple_of(dst_row * SUBLANES, SUBLANES), SUBLANES)], sem)


def _params(*sem):
    return pltpu.CompilerParams(dimension_semantics=sem, vmem_limit_bytes=VMEM_LIMIT)


def _rmsnorm_kernel(x_ref, g_ref, o_ref):
    x = x_ref[...]
    ms = jnp.mean(x * x, axis=-1, keepdims=True)
    o_ref[...] = (x * lax.rsqrt(ms + EPS) * g_ref[...]).astype(o_ref.dtype)


def rmsnorm_rows(x, g, tm):
    n, d = x.shape
    return pl.pallas_call(
        _rmsnorm_kernel,
        out_shape=jax.ShapeDtypeStruct((n, d), BF16),
        grid=(n // tm,),
        in_specs=[pl.BlockSpec((tm, d), lambda i: (i, 0)),
                  pl.BlockSpec((1, d), lambda i: (0, 0))],
        out_specs=pl.BlockSpec((tm, d), lambda i: (i, 0)),
        compiler_params=_params("parallel"),
        name="rmsnorm_rows",
    )(x, g.reshape(1, d))


def _proj_kernel(h_ref, w_ref, o_ref):
    o_ref[...] = _dot(h_ref[...], w_ref[...]).astype(o_ref.dtype)


def _proj_groupnorm_kernel(h_ref, w_ref, gsum_ref, gexp_ref, gain_ref, o_ref):
    z = _dot(h_ref[...], w_ref[...])
    ms = _dot((z * z).astype(BF16), gsum_ref[...])
    inv = lax.rsqrt(ms + EPS)
    inv_hi = inv.astype(BF16)
    inv_lo = (inv - inv_hi.astype(F32)).astype(BF16)
    inv_full = _dot(inv_hi, gexp_ref[...]) + _dot(inv_lo, gexp_ref[...])
    o_ref[...] = (z * inv_full * gain_ref[...]).astype(o_ref.dtype)


def project(h, w, out_dtype, tn, group_gain=None, group=A_HD, row0=0, n_rows=None):
    k = h.shape[1]
    n = h.shape[0] if n_rows is None else n_rows
    m = w.shape[1]
    tm = _largest_block(math.gcd(n, row0) if row0 else n, PROJ_ROWS)
    grid = (m // tn, n // tm)
    blk0 = row0 // tm
    h_spec = pl.BlockSpec((tm, k), lambda j, i: (blk0 + i, 0))
    w_spec = pl.BlockSpec((k, tn), lambda j, i: (0, j))
    o_spec = pl.BlockSpec((tm, tn), lambda j, i: (i, j))
    out_shape = jax.ShapeDtypeStruct((n, m), out_dtype)
    if group_gain is None:
        return pl.pallas_call(
            _proj_kernel, out_shape=out_shape, grid=grid,
            in_specs=[h_spec, w_spec], out_specs=o_spec,
            compiler_params=_params("parallel", "parallel"), name="project",
        )(h, w)
    ng = tn // group
    cols = np.arange(tn) // group
    gsum = jnp.asarray((cols[:, None] == np.arange(ng)[None, :]) / group, BF16)
    gexp = jnp.asarray(np.arange(ng)[:, None] == cols[None, :], BF16)
    return pl.pallas_call(
        _proj_groupnorm_kernel, out_shape=out_shape, grid=grid,
        in_specs=[h_spec, w_spec,
                  pl.BlockSpec((tn, ng), lambda j, i: (0, 0)),
                  pl.BlockSpec((ng, tn), lambda j, i: (0, 0)),
                  pl.BlockSpec((1, tn), lambda j, i: (0, j))],
        out_specs=o_spec,
        compiler_params=_params("parallel", "parallel"), name="project_groupnorm",
    )(h, w, gsum, gexp, group_gain.reshape(1, m).astype(F32))


def _map_masks(q):
    lane = lax.broadcasted_iota(jnp.int32, q.shape, 1)
    zero = jnp.zeros_like(q)
    return (jnp.where(lane < A_HD, q, zero), jnp.where(lane >= A_HD, q, zero))


def _subln(o, g_ref, post_scale):
    ms = jnp.mean(o * o, axis=-1, keepdims=True)
    return o * lax.rsqrt(ms + EPS) * g_ref[...] * post_scale


def _attn_prompt_kernel(lam_ref, slope_ref, q_ref, k_ref, v_ref, g_ref, o_ref, kb, vb, bias_tab, s_scr,
                        *, seq_len, post_scale):
    tb = ATT_BLOCK
    n_frames = seq_len - N_META
    n_blocks = n_frames // tb
    slope = slope_ref[pl.program_id(1)] * LOG2E
    lam = lam_ref[0]
    kb[...] = k_ref[...].astype(BF16)
    vb[...] = v_ref[...].astype(BF16)

    qm = _map_masks(q_ref[0:N_META, :])
    r = lax.broadcasted_iota(jnp.int32, (N_META, N_META), 0)
    c = lax.broadcasted_iota(jnp.int32, (N_META, N_META), 1)
    bias = -slope * jnp.abs(r - c).astype(F32)
    km = kb[0:N_META, :]
    vm = vb[0:N_META, :]
    ws = []
    for m in range(2):
        s = _dot_nt(qm[m], km) + bias
        p = jnp.exp2(s - jnp.max(s, axis=-1, keepdims=True))
        ws.append(p / jnp.sum(p, axis=-1, keepdims=True))
    w = ws[0] - lam * ws[1]
    o_ref[0:N_META, :] = _subln(_dot(w.astype(BF16), vm), g_ref, post_scale).astype(o_ref.dtype)

    for c0 in range(0, n_frames, ATT_TABLE_COLS):
        wd = min(ATT_TABLE_COLS, n_frames - c0)
        r = lax.broadcasted_iota(jnp.int32, (tb, wd), 0)
        u = lax.broadcasted_iota(jnp.int32, (tb, wd), 1) + (c0 - (n_frames - tb))
        hidden = u >= (r // CHUNK + 1) * CHUNK
        bias_tab[:, c0:c0 + wd] = jnp.where(hidden, NEG_INF, -slope * jnp.abs(r - u).astype(F32))
    rm = lax.broadcasted_iota(jnp.int32, (tb, N_META), 0)
    cm = lax.broadcasted_iota(jnp.int32, (tb, N_META), 1)
    meta_bias0 = -slope * (N_META + rm - cm).astype(F32)

    def lane_tiles(x):
        return [x[:, t * LANES:(t + 1) * LANES] for t in range(x.shape[1] // LANES)]

    def key_chunks(i):
        n_keys = tb * (i + 1)
        return [(c0, min(ATT_KEY_COLS, n_keys - c0)) for c0 in range(0, n_keys, ATT_KEY_COLS)]

    def score_pass(i, m, qm):
        off = n_frames - tb * (i + 1)
        scr = s_scr.at[i % 2, m]
        s_meta = _dot_nt(qm[m], km) + (meta_bias0 - slope * float(tb * i))
        mx = None
        for c0, wd in key_chunks(i):
            s = (_dot_nt(qm[m], kb[N_META + c0:N_META + c0 + wd, :])
                 + bias_tab[:, off + c0:off + c0 + wd])
            scr[:, c0:c0 + wd] = s
            for t in lane_tiles(s):
                mx = t if mx is None else jnp.maximum(mx, t)
        m_row = jnp.maximum(jnp.max(mx, axis=-1, keepdims=True), jnp.max(s_meta, axis=-1, keepdims=True))
        return s_meta, m_row

    def value_pass(i, m, s_meta, m_row):
        scr = s_scr.at[i % 2, m]
        p_meta = jnp.exp2(s_meta - m_row)
        acc = _dot(p_meta.astype(BF16), vm)
        lsum = None
        for c0, wd in key_chunks(i):
            p = jnp.exp2(scr[:, c0:c0 + wd] - m_row)
            for t in lane_tiles(p):
                lsum = t if lsum is None else lsum + t
            acc = acc + _dot(p.astype(BF16), vb[N_META + c0:N_META + c0 + wd, :])
        l_row = jnp.sum(lsum, axis=-1, keepdims=True) + jnp.sum(p_meta, axis=-1, keepdims=True)
        return acc / l_row

    units = [(i, m) for i in range(n_blocks) for m in range(2)]
    qms, outs = {}, {}

    def finish(i, m, s_meta, m_row):
        outs[m] = value_pass(i, m, s_meta, m_row)
        if m == 1:
            q0 = N_META + i * tb
            o = outs[0] - lam * outs[1]
            o_ref[q0:q0 + tb, :] = _subln(o, g_ref, post_scale).astype(o_ref.dtype)

    pending = None
    for i, m in units:
        if m == 0:
            q0 = N_META + i * tb
            qms[i] = _map_masks(q_ref[q0:q0 + tb, :])
        scored = score_pass(i, m, qms[i])
        if pending is not None:
            finish(*pending)
        pending = (i, m) + scored
    finish(*pending)


def attention_prompt(q, k, v, lam, slopes, subln_g, batch, seq_len, heads, row_block0, post_scale):
    hw = 2 * A_HD
    spec = lambda: pl.BlockSpec((seq_len, hw), lambda b, h, *_: (row_block0 + b, h))
    kern = functools.partial(_attn_prompt_kernel, seq_len=seq_len, post_scale=post_scale)
    return pl.pallas_call(
        kern,
        out_shape=jax.ShapeDtypeStruct((batch * seq_len, heads * hw), BF16),
        grid_spec=pltpu.PrefetchScalarGridSpec(
            num_scalar_prefetch=2, grid=(batch, heads),
            in_specs=[spec(), spec(), spec(), pl.BlockSpec((1, hw), lambda b, h, *_: (0, 0))],
            out_specs=pl.BlockSpec((seq_len, hw), lambda b, h, *_: (b, h)),
            scratch_shapes=[pltpu.VMEM((seq_len, hw), BF16), pltpu.VMEM((seq_len, hw), BF16),
                            pltpu.VMEM((ATT_BLOCK, seq_len - N_META), F32),
                            pltpu.VMEM((2, 2, ATT_BLOCK, seq_len - N_META), F32)]),
        compiler_params=_params("parallel", "parallel"), name="attention_prompt",
    )(lam, slopes, q, k, v, subln_g.reshape(1, hw))


def _attn_sample_kernel(lam_ref, slope_ref, q_ref, k_ref, v_ref, ck_ref, cv_ref, g_ref, o_ref,
                        *, heads, post_scale):
    lam = lam_ref[0]
    t = q_ref.shape[0]
    hw = 2 * A_HD
    past = ck_ref.shape[1] // heads
    r = lax.broadcasted_iota(jnp.int32, (t, past), 0)
    c = lax.broadcasted_iota(jnp.int32, (t, past), 1)
    dist_c = (past + r - c).astype(F32)
    r = lax.broadcasted_iota(jnp.int32, (t, t), 0)
    c = lax.broadcasted_iota(jnp.int32, (t, t), 1)
    dist_n = jnp.abs(r - c).astype(F32)
    scores, vals = [], []
    for h in range(heads):
        lanes = slice(h * hw, (h + 1) * hw)
        slope = slope_ref[h] * LOG2E
        qm = _map_masks(q_ref[:, lanes])
        kn = k_ref[:, lanes].astype(BF16)
        kc = ck_ref[0, pl.ds(h, past, stride=heads), :].astype(BF16)
        scores.append([(_dot_nt(qm[m], kc) - slope * dist_c, _dot_nt(qm[m], kn) - slope * dist_n)
                       for m in range(2)])
    for h in range(heads):
        wc, wn = [], []
        for sc, sn in scores[h]:
            mx = jnp.maximum(jnp.max(sc, axis=-1, keepdims=True), jnp.max(sn, axis=-1, keepdims=True))
            pc = jnp.exp2(sc - mx)
            pn = jnp.exp2(sn - mx)
            den = jnp.sum(pc, axis=-1, keepdims=True) + jnp.sum(pn, axis=-1, keepdims=True)
            wc.append(pc / den)
            wn.append(pn / den)
        vals.append(((wc[0] - lam * wc[1]).astype(BF16), (wn[0] - lam * wn[1]).astype(BF16)))
    for h in range(heads):
        lanes = slice(h * hw, (h + 1) * hw)
        vc = cv_ref[0, pl.ds(h, past, stride=heads), :].astype(BF16)
        o = _dot(vals[h][0], vc) + _dot(vals[h][1], v_ref[:, lanes].astype(BF16))
        o_ref[:, lanes] = _subln(o, g_ref, post_scale).astype(o_ref.dtype)


def attention_sample(q, k, v, cache_k, cache_v, cache_block0, lam, slopes, subln_g, batch, t, heads,
                     row_block0, post_scale):
    hw = 2 * A_HD
    d = heads * hw
    rows = cache_k.shape[1]
    new_q = pl.BlockSpec((t, d), lambda b, *_: (row_block0 + b, 0))
    new = lambda: pl.BlockSpec((t, d), lambda b, *_: (b, 0))
    cache = lambda: pl.BlockSpec((1, rows, hw), lambda b, *_: (cache_block0 + b, 0, 0))
    kern = functools.partial(_attn_sample_kernel, heads=heads, post_scale=post_scale)
    return pl.pallas_call(
        kern,
        out_shape=jax.ShapeDtypeStruct((batch * t, d), BF16),
        grid_spec=pltpu.PrefetchScalarGridSpec(
            num_scalar_prefetch=2, grid=(batch,),
            in_specs=[new_q, new(), new(), cache(), cache(),
                      pl.BlockSpec((1, hw), lambda b, *_: (0, 0))],
            out_specs=pl.BlockSpec((t, d), lambda b, *_: (b, 0))),
        compiler_params=pltpu.CompilerParams(dimension_semantics=("parallel",),
                                             vmem_limit_bytes=SAMPLE_VMEM_LIMIT),
        name="attention_sample",
    )(lam, slopes, q, k, v, cache_k, cache_v, subln_g.reshape(1, hw))


def _hgrn_prepare(start, size, heads_per_step, refs, tri):
    q_ref, f_ref, i_ref, g_ref, lb_ref, ng_ref, o_ref = refs
    nsub = size // REC_SUB
    rows = pl.ds(start, size)
    width = heads_per_step * LANES
    row = lax.broadcasted_iota(jnp.int32, (size, width), 0)
    lb = lb_ref[...]
    fg = lb + (1.0 - lb) * jax.nn.sigmoid(f_ref[rows, :])
    kk = 1.0 - fg
    logf = jnp.log(fg)
    qq = jax.nn.silu(q_ref[rows, :].astype(F32))
    vv = i_ref[rows, :].astype(BF16)
    lf_hi = logf.astype(BF16)
    lf_lo = (logf - lf_hi.astype(F32)).astype(BF16)
    b = _dot(tri, lf_hi) + _dot(tri, lf_lo)
    r_parts = [jnp.zeros((REC_SUB, width), F32)]
    for s in range(1, nsub):
        r_parts.append(jnp.broadcast_to(b[s * REC_SUB - 1:s * REC_SUB, :], (REC_SUB, width)))
    r_sub = r_parts[0] if nsub == 1 else jnp.concatenate(r_parts, axis=0)
    q_sub = qq * jnp.exp(b - r_sub)
    q_dec = (q_sub * jnp.exp(r_sub)).astype(BF16)
    b_last = b[size - 1:size, :]
    k_end = (kk * jnp.exp(b_last - b)).astype(BF16)
    q_parts, k_parts = [], []
    for i in range(nsub):
        in_sub = (row >= i * REC_SUB) & (row < (i + 1) * REC_SUB)
        q_parts.append(jnp.where(in_sub, q_sub, 0.0).astype(BF16))
        rel = jnp.where(row < (i + 1) * REC_SUB, r_sub[i * REC_SUB:i * REC_SUB + 1, :] - b, NEG_INF)
        k_parts.append((kk * jnp.exp(rel)).astype(BF16))
    return dict(rows=rows, size=size, q_dec=q_dec, q_parts=q_parts, k_parts=k_parts, k_end=k_end,
                vv=vv, decay=jnp.exp(b_last))


def _hgrn_scores(prep, p):
    lanes = slice(p * LANES, (p + 1) * LANES)
    size = prep["size"]
    q_hat = jnp.concatenate([x[:, lanes] for x in prep["q_parts"]], axis=1)
    k_hat = jnp.concatenate([x[:, lanes] for x in prep["k_parts"]], axis=1)
    rr = lax.broadcasted_iota(jnp.int32, (size, size), 0)
    cc = lax.broadcasted_iota(jnp.int32, (size, size), 1)
    a = jnp.where(cc <= rr, _dot_nt(q_hat, k_hat), 0.0).astype(BF16)
    kv = _dot_tn(prep["k_end"][:, lanes], prep["vv"][:, lanes])
    return a, kv


def _hgrn_finish(prep, scores, p, refs, st_ref):
    q_ref, f_ref, i_ref, g_ref, lb_ref, ng_ref, o_ref = refs
    lanes = slice(p * LANES, (p + 1) * LANES)
    rows = prep["rows"]
    a, kv = scores
    st = st_ref[p]
    o = _dot(jnp.concatenate([prep["q_dec"][:, lanes], a], axis=1),
             jnp.concatenate([st.astype(BF16), prep["vv"][:, lanes]], axis=0))
    ms = jnp.mean(o * o, axis=-1, keepdims=True)
    on = o * lax.rsqrt(ms + EPS) * ng_ref[...]
    gg = g_ref[rows, lanes].astype(F32)
    o_ref[rows, lanes] = (on * jax.nn.silu(gg)).astype(o_ref.dtype)
    decay_cols = jnp.broadcast_to(prep["decay"][:, lanes], (LANES, LANES)).T
    st_ref[p] = st * decay_cols + kv


def _hgrn_chunks(starts, size, heads_per_step, refs, st_ref, tri):
    preps = [_hgrn_prepare(s, size, heads_per_step, refs, tri) for s in starts]
    scores = [[_hgrn_scores(prep, p) for p in range(heads_per_step)] for prep in preps]
    for prep, sc in zip(preps, scores):
        for p in range(heads_per_step):
            _hgrn_finish(prep, sc[p], p, refs, st_ref)


def _hgrn_consts(size):
    r = lax.broadcasted_iota(jnp.int32, (size, size), 0)
    c = lax.broadcasted_iota(jnp.int32, (size, size), 1)
    return jnp.where(c <= r, 1.0, 0.0).astype(BF16)


def _hgrn_kernel(*args, seq_len, heads_per_step, has_state):
    if has_state:
        q_ref, f_ref, i_ref, g_ref, lb_ref, ng_ref, s0_ref, o_ref, s_ref, st_ref = args
    else:
        q_ref, f_ref, i_ref, g_ref, lb_ref, ng_ref, o_ref, s_ref, st_ref = args
    refs = (q_ref, f_ref, i_ref, g_ref, lb_ref, ng_ref, o_ref)
    for p in range(heads_per_step):
        if has_state:
            st_ref[p] = s0_ref[0, p]
        else:
            st_ref[p] = jnp.zeros(st_ref.shape[1:], F32)
    _hgrn_chunks([0], N_META, heads_per_step, refs, st_ref, _hgrn_consts(N_META))
    n_chunks = (seq_len - N_META) // CHUNK
    if n_chunks:
        tri = _hgrn_consts(CHUNK)
        per_step = 2 if n_chunks % 2 == 0 else 1

        def body(ci, _):
            starts = [pl.multiple_of(N_META + (ci * per_step + j) * CHUNK, N_META) for j in range(per_step)]
            _hgrn_chunks(starts, CHUNK, heads_per_step, refs, st_ref, tri)
            return 0

        lax.fori_loop(0, n_chunks // per_step, body, 0)
    for p in range(heads_per_step):
        s_ref[0, p] = st_ref[p]


def hgrn2(zr, f_b, lb, norm_g, s0, batch, seq_len, heads, row_block0, heads_per_step):
    dk = LANES
    hpw = heads_per_step * dk
    d = heads * dk
    nhp = heads // heads_per_step
    col = lambda seg: pl.BlockSpec((seq_len, hpw), lambda b, hp, seg=seg: (row_block0 + b, seg * nhp + hp))
    in_specs = [col(0), pl.BlockSpec((seq_len, hpw), lambda b, hp: (row_block0 + b, hp)), col(1), col(2),
                pl.BlockSpec((1, hpw), lambda b, hp: (0, hp)),
                pl.BlockSpec((1, dk), lambda b, hp: (0, 0))]
    args = [zr, f_b, zr, zr, lb.reshape(1, d), norm_g.reshape(1, dk)]
    state_spec = pl.BlockSpec((1, heads_per_step, dk, dk), lambda b, hp: (b, hp, 0, 0))
    if s0 is not None:
        in_specs.append(state_spec)
        args.append(s0)
    kern = functools.partial(_hgrn_kernel, seq_len=seq_len, heads_per_step=heads_per_step,
                             has_state=s0 is not None)
    return pl.pallas_call(
        kern,
        out_shape=(jax.ShapeDtypeStruct((batch * seq_len, d), BF16),
                   jax.ShapeDtypeStruct((batch, heads, dk, dk), F32)),
        grid=(batch, nhp),
        in_specs=in_specs,
        out_specs=(pl.BlockSpec((seq_len, hpw), lambda b, hp: (b, hp)), state_spec),
        scratch_shapes=[pltpu.VMEM((heads_per_step, dk, dk), F32)],
        compiler_params=_params("parallel", "parallel"), name="hgrn2",
    )(*args)


def _merge_kernel(oap_ref, oas_ref, obp_ref, obs_ref, ga_ref, gb_ref, x_ref, wa_ref, wb_ref, wo_ref, g2_ref,
                  wr_ref, br_ref, x1_ref, h2_ref, lg_ref, *, prompt_blocks):
    is_prompt = pl.program_id(0) < prompt_blocks
    ya = _dot(jnp.where(is_prompt, oap_ref[...], oas_ref[...]), wa_ref[...])
    yb = _dot(jnp.where(is_prompt, obp_ref[...], obs_ref[...]), wb_ref[...])
    mix = jax.nn.sigmoid(ga_ref[...].astype(F32)) * ya + jax.nn.sigmoid(gb_ref[...].astype(F32)) * yb
    x1 = x_ref[...] + _dot(mix.astype(BF16), wo_ref[...])
    x1_ref[...] = x1
    ms = jnp.mean(x1 * x1, axis=-1, keepdims=True)
    h2 = x1 * lax.rsqrt(ms + EPS) * g2_ref[...]
    _store_row_tiles(h2_ref, h2)
    ne = lg_ref.shape[1]
    h_hi = h2.astype(BF16)
    h_lo = (h2 - h_hi.astype(F32)).astype(BF16)
    both = _dot(h_hi, wr_ref[...])
    lg_ref[...] = both[:, :ne] + both[:, ne:] + _dot(h_lo, wr_ref[:, :ne]) + br_ref[...]


def merge_branches(oa_p, oa_s, ob_p, ob_s, zr, x, wa, wb, wo, g2, w_router, b_router):
    n, d = x.shape
    ne = w_router.shape[1]
    n_p, n_s = oa_p.shape[0], oa_s.shape[0]
    tm = _largest_block(math.gcd(n_p, n_s), MERGE_ROWS)
    nbp = n_p // tm
    row = lambda: pl.BlockSpec((tm, d), lambda i: (i, 0))
    prompt = lambda: pl.BlockSpec((tm, d), lambda i: (jnp.minimum(i, nbp - 1), 0))
    sample = lambda: pl.BlockSpec((tm, d), lambda i: (jnp.maximum(i - nbp, 0), 0))
    full = lambda a: pl.BlockSpec(a.shape, lambda i: (0, 0))
    g2 = g2.reshape(1, d)
    b_router = b_router.reshape(1, ne)
    w_hi = w_router.astype(BF16)
    w_router = jnp.concatenate([w_hi, (w_router - w_hi.astype(F32)).astype(BF16)], axis=1)
    return pl.pallas_call(
        functools.partial(_merge_kernel, prompt_blocks=nbp),
        out_shape=(jax.ShapeDtypeStruct((n, d), F32), jax.ShapeDtypeStruct((n * SUBLANES, LANES), F32),
                   jax.ShapeDtypeStruct((n, ne), F32)),
        grid=(n // tm,),
        in_specs=[prompt(), sample(), prompt(), sample(),
                  pl.BlockSpec((tm, d), lambda i: (i, 3)), pl.BlockSpec((tm, d), lambda i: (i, 4)),
                  row(), full(wa), full(wb), full(wo), full(g2), full(w_router), full(b_router)],
        out_specs=(row(), pl.BlockSpec((tm * SUBLANES, LANES), lambda i: (i, 0)),
                   pl.BlockSpec((tm, ne), lambda i: (i, 0))),
        compiler_params=_params("parallel"), name="merge_branches",
    )(oa_p, oa_s, ob_p, ob_s, zr, zr, x, wa, wb, wo, g2, w_router, b_router)


def _deinterleave_kernel(w_ref, perm_ref, g_ref, u_ref):
    perm = perm_ref[...]
    half = MXU_DIM // 2
    for t in range(w_ref.shape[2] // MXU_DIM):
        y = _dot(w_ref[0, :, t * MXU_DIM:(t + 1) * MXU_DIM].astype(BF16), perm)
        g_ref[0, :, t * half:(t + 1) * half] = y[:, :half].astype(BF16)
        u_ref[0, :, t * half:(t + 1) * half] = y[:, half:].astype(BF16)


def deinterleave_gate_up(w1, rows):
    e, k, n2 = w1.shape
    half = MXU_DIM // 2
    idx = np.arange(MXU_DIM)
    perm = np.zeros((MXU_DIM, MXU_DIM), np.float32)
    perm[idx, np.where(idx % 2 == 0, idx // 2, half + idx // 2)] = 1.0
    out = jax.ShapeDtypeStruct((e, k, n2 // 2), BF16)
    out_spec = lambda: pl.BlockSpec((1, rows, n2 // 2), lambda i, j: (i, j, 0))
    return pl.pallas_call(
        _deinterleave_kernel, out_shape=(out, out), grid=(e, k // rows),
        in_specs=[pl.BlockSpec((1, rows, n2), lambda i, j: (i, j, 0)),
                  pl.BlockSpec((MXU_DIM, MXU_DIM), lambda i, j: (0, 0))],
        out_specs=(out_spec(), out_spec()),
        compiler_params=_params("parallel", "parallel"), name="deinterleave_gate_up",
    )(w1, jnp.asarray(perm, BF16))


DISPATCH_RING = 3


def _moe_dispatch_kernel(slot_ref, h_hbm, o_hbm, stage, sem_in, sem_out):
    i = pl.program_id(0)
    nb = pl.num_programs(0)
    tm = slot_ref.shape[2] // TOP_K
    rows = tm * SUBLANES

    def load(step):
        s = lax.rem(step, DISPATCH_RING)
        src = h_hbm.at[pl.ds(pl.multiple_of(step * rows, rows), rows)]
        return pltpu.make_async_copy(src, stage.at[s], sem_in.at[s])

    def wait_scatter(step):
        s = lax.rem(step, DISPATCH_RING)

        def body(r, _):
            for k in range(TOP_K):
                _tile_copy(stage.at[s], r, o_hbm, 0, sem_out.at[s]).wait()
            return 0
        lax.fori_loop(0, tm, body, 0, unroll=4)

    @pl.when(i == 0)
    def _():
        load(i).start()

    @pl.when(i >= DISPATCH_RING - 1)
    def _():
        wait_scatter(i - (DISPATCH_RING - 1))

    @pl.when(i + 1 < nb)
    def _():
        load(i + 1).start()

    load(i).wait()
    cur = lax.rem(i, DISPATCH_RING)

    def body(r, _):
        for k in range(TOP_K):
            _tile_copy(stage.at[cur], r, o_hbm, slot_ref[0, 0, k * tm + r],
                       sem_out.at[cur]).start(priority=k % 2)
        return 0
    lax.fori_loop(0, tm, body, 0, unroll=2)

    @pl.when(i == nb - 1)
    def _():
        for back in range(DISPATCH_RING - 2, -1, -1):
            @pl.when(i - back >= 0)
            def _(back=back):
                wait_scatter(i - back)


def moe_dispatch(h2_tiles, slots, tm):
    n = slots.shape[0]
    nb = n // tm
    ids_arr = slots.reshape(nb, tm, TOP_K).transpose(0, 2, 1).reshape(nb, 1, TOP_K * tm)
    return pl.pallas_call(
        _moe_dispatch_kernel,
        out_shape=jax.ShapeDtypeStruct((n * TOP_K * SUBLANES, LANES), F32),
        grid=(nb,),
        in_specs=[pl.BlockSpec((1, 1, TOP_K * tm), lambda i: (i, 0, 0), memory_space=pltpu.SMEM),
                  pl.BlockSpec(memory_space=pl.ANY)],
        out_specs=pl.BlockSpec(memory_space=pl.ANY),
        scratch_shapes=[pltpu.VMEM((DISPATCH_RING, tm * SUBLANES, LANES), F32),
                        pltpu.SemaphoreType.DMA((DISPATCH_RING,)),
                        pltpu.SemaphoreType.DMA((DISPATCH_RING,))],
        compiler_params=_params("arbitrary"), name="moe_dispatch",
    )(ids_arr, h2_tiles)


def _moe_ffn_kernel(blk_ref, exp_ref, lo_ref, hi_ref, first_ref, x_ref, w1g_ref, w1u_ref, b1g_ref,
                    b1u_ref, w2_ref, b2_ref, o_ref):
    i = pl.program_id(0)
    tm = x_ref.shape[0] // SUBLANES
    lo = lo_ref[i]
    hi = hi_ref[i]

    @pl.when(hi > lo)
    def _():
        x = _load_row_tiles(x_ref, tm).astype(BF16)
        gate = jnp.minimum(_dot(x, w1g_ref[0]) + b1g_ref[0], SWIGLU_LIMIT)
        up = jnp.clip(_dot(x, w1u_ref[0]) + b1u_ref[0], -SWIGLU_LIMIT, SWIGLU_LIMIT)
        act = gate * jax.nn.sigmoid(SWIGLU_ALPHA * gate) * (up + 1.0)
        y = _dot(act.astype(BF16), w2_ref[0]) + b2_ref[0]
        row = lax.broadcasted_iota(jnp.int32, (tm, LANES), 0)
        mine = (row >= lo) & (row < hi)

        def write(keep_old):
            for s in range(SUBLANES):
                dst = (pl.ds(s, tm, stride=SUBLANES), slice(None))
                old = o_ref[dst] if keep_old else 0.0
                o_ref[dst] = jnp.where(mine, y[:, s * LANES:(s + 1) * LANES], old)

        pl.when(first_ref[i] == 1)(functools.partial(write, False))
        pl.when(first_ref[i] == 0)(functools.partial(write, True))


def moe_ffn(x_tiles, items, w1g, w1u, b1g, b1u, w2, b2, tm):
    n_items = items[0].shape[0]
    wspec = lambda a: pl.BlockSpec((1,) + a.shape[1:], lambda i, blk, exp, *_: (exp[i], 0, 0))
    rows = lambda: pl.BlockSpec((tm * SUBLANES, LANES), lambda i, blk, *_: (blk[i], 0))
    return pl.pallas_call(
        _moe_ffn_kernel,
        out_shape=jax.ShapeDtypeStruct(x_tiles.shape, F32),
        grid_spec=pltpu.PrefetchScalarGridSpec(
            num_scalar_prefetch=5, grid=(n_items,),
            in_specs=[rows(), wspec(w1g), wspec(w1u), wspec(b1g), wspec(b1u), wspec(w2), wspec(b2)],
            out_specs=rows()),
        compiler_params=_params("arbitrary"), name="moe_ffn",
    )(*items, x_tiles, w1g, w1u, b1g, b1u, w2, b2)


def _moe_combine_kernel(slot_ref, slot_next_ref, gate_ref, x_ref, rows_hbm, o_ref, buf, sem):
    i = pl.program_id(0)
    nb = pl.num_programs(0)
    tm = buf.shape[2] // SUBLANES
    slot = lax.rem(i, 2)

    def start_rows(ids_ref, s):
        def body(r, _):
            for k in range(TOP_K):
                _tile_copy(rows_hbm, ids_ref[0, 0, k * tm + r], buf.at[s, k], r,
                           sem.at[s]).start(priority=k % 2)
            return 0
        lax.fori_loop(0, tm, body, 0, unroll=2)

    @pl.when(i == 0)
    def _():
        start_rows(slot_ref, 0)

    @pl.when(i + 1 < nb)
    def _():
        start_rows(slot_next_ref, 1 - slot)

    def wait_body(r, _):
        for k in range(TOP_K):
            _tile_copy(rows_hbm, 0, buf.at[slot, k], r, sem.at[slot]).wait()
        return 0
    lax.fori_loop(0, tm, wait_body, 0, unroll=4)

    g = gate_ref[...]
    gk = [g[:, k:k + 1] for k in range(TOP_K)]
    for s in range(SUBLANES):
        piece = pl.ds(s, tm, stride=SUBLANES)
        y = buf[slot, 0, piece, :] * gk[0]
        for k in range(1, TOP_K):
            y = y + buf[slot, k, piece, :] * gk[k]
        o_ref[:, s * LANES:(s + 1) * LANES] = x_ref[:, s * LANES:(s + 1) * LANES] + y


def moe_combine(rows, slots, gates, x1, tm):
    n, d = x1.shape
    nb = n // tm
    ids_arr = slots.reshape(nb, tm, TOP_K).transpose(0, 2, 1).reshape(nb, 1, TOP_K * tm)
    ids = lambda f: pl.BlockSpec((1, 1, TOP_K * tm), f, memory_space=pltpu.SMEM)
    return pl.pallas_call(
        _moe_combine_kernel,
        out_shape=jax.ShapeDtypeStruct((n, d), F32),
        grid=(nb,),
        in_specs=[ids(lambda i: (i, 0, 0)),
                  ids(lambda i: (jnp.minimum(i + 1, nb - 1), 0, 0)),
                  pl.BlockSpec((tm, TOP_K), lambda i: (i, 0)),
                  pl.BlockSpec((tm, d), lambda i: (i, 0)),
                  pl.BlockSpec(memory_space=pl.ANY)],
        out_specs=pl.BlockSpec((tm, d), lambda i: (i, 0)),
        scratch_shapes=[pltpu.VMEM((2, TOP_K, tm * SUBLANES, LANES), F32), pltpu.SemaphoreType.DMA((2,))],
        compiler_params=_params("arbitrary"), name="moe_combine",
    )(ids_arr, ids_arr, gates, x1, rows)


def _route(logits, n_experts, tm, tb):
    n = logits.shape[0]
    nk = n * TOP_K
    i32 = jnp.int32
    top_val, top_idx = lax.top_k(logits, TOP_K)
    gates = jax.nn.softmax(top_val, axis=-1)
    eid = top_idx.astype(i32)
    chosen = (eid[:, :, None] == jnp.arange(n_experts, dtype=i32)[None, None, :]).any(axis=1)
    nb = n // tb
    cb = chosen.reshape(nb, tb, n_experts).astype(BF16)
    before = jnp.asarray(np.tril(np.ones((tb, tb), np.float32), -1), BF16)
    within = jnp.einsum('ts,bse->bte', before, cb, preferred_element_type=F32)
    per_block = jnp.sum(cb.astype(F32), axis=1)
    block_off = jnp.cumsum(per_block, axis=0) - per_block
    rank = (within + block_off[:, None, :]).reshape(n, n_experts)
    counts = jnp.sum(per_block, axis=0).astype(i32)
    gend = jnp.cumsum(counts)
    gstart = gend - counts
    slots = gstart[eid] + jnp.take_along_axis(rank, eid, axis=1).astype(i32)

    n_blk = nk // tm
    n_items = n_blk + n_experts - 1
    first_blk = gstart // tm
    tiles = jnp.where(counts > 0, (gend - 1) // tm - first_blk + 1, 0)
    tile_end = jnp.cumsum(tiles)
    idx = jnp.arange(n_items, dtype=i32)
    valid = idx < tile_end[-1]
    exp = jnp.minimum(jnp.sum((tile_end[None, :] <= idx[:, None]).astype(i32), axis=1), n_experts - 1)
    blk = jnp.where(valid, first_blk[exp] + idx - (tile_end[exp] - tiles[exp]), n_blk - 1)
    lo = jnp.where(valid, jnp.clip(gstart[exp] - blk * tm, 0, tm), 0)
    hi = jnp.where(valid, jnp.clip(gend[exp] - blk * tm, 0, tm), 0)
    first = (blk != jnp.concatenate([jnp.full((1,), -1, i32), blk[:-1]])).astype(i32)
    return gates, slots, (blk.astype(i32), exp, lo.astype(i32), hi.astype(i32), first)


def kernel(x_prompt, x_sample, cache_k, cache_v, state_hgrn, meta_tokens, norm_mix_g, w_in, q_norm_g,
           k_norm_g, lambda_q1, lambda_k1, lambda_q2, lambda_k2, subln_g, hgrn_lb, hgrn_norm_g, w_a, w_b,
           w_o, norm_ffn_g, w_router, b_router, w_mlp1, b_mlp1, w_mlp2, b_mlp2):
    batch, seq, d = x_prompt.shape
    dec_batch, dec_seq, _ = x_sample.shape
    depth = w_in.shape[0]
    heads = cache_k.shape[3]
    b_heads = state_hgrn.shape[2]
    n_experts = w_router.shape[-1]
    d_ff = w_mlp2.shape[2]
    lp = N_META + seq
    n_p = batch * lp
    n_s = dec_batch * dec_seq
    n = n_p + n_s
    assert dec_seq == N_META and seq % ATT_BLOCK == 0 and n_p % dec_seq == 0
    assert d == heads * 2 * A_HD == b_heads * LANES == SUBLANES * LANES

    meta = jnp.broadcast_to(meta_tokens[None].astype(x_prompt.dtype), (batch, N_META, d))
    x = jnp.concatenate([jnp.concatenate([meta, x_prompt], axis=1).reshape(n_p, d),
                         x_sample.reshape(n_s, d)], axis=0)

    tm = _pick_block(n, (512, 256, 128, 64, 32, 16))
    tm_moe = _pick_block(n * TOP_K, (256, 128, 64))
    tm_tok = _pick_block(n, (256, 128, 64, 32, 16))
    tm_comb = _pick_block(n, (256, 128, 64, 32, 16))
    slopes = jnp.asarray(np.power(2.0, -8.0 * np.arange(1, heads + 1) / heads).astype(np.float32))
    lower_bounds = jnp.cumsum(jax.nn.softmax(hgrn_lb.astype(F32), axis=0), axis=0)

    k_p, v_p, s_p, k_s, v_s, s_s = [], [], [], [], [], []
    for l in range(depth):
        lam_init = 0.8 - 0.6 * math.exp(-0.3 * l)
        lam = (jnp.exp(jnp.sum(lambda_q1[l].astype(F32) * lambda_k1[l].astype(F32)))
               - jnp.exp(jnp.sum(lambda_q2[l].astype(F32) * lambda_k2[l].astype(F32))) + lam_init).reshape(1)
        w = w_in[l].astype(BF16)
        seg = lambda j: w[:, j * d:(j + 1) * d]

        h = rmsnorm_rows(x, norm_mix_g[l], tm)
        q_a = project(h, seg(0), BF16, d,
                      group_gain=jnp.tile(q_norm_g[l].astype(F32), d // A_HD) * (A_HD ** -0.5 * LOG2E))
        k_gain = jnp.tile(k_norm_g[l].astype(F32), d // A_HD)
        k_pr = project(h, seg(1), F32, d, group_gain=k_gain, n_rows=n_p)
        k_sm = project(h, seg(1), F32, d, group_gain=k_gain, row0=n_p, n_rows=n_s)
        v_pr = project(h, seg(2), F32, d, n_rows=n_p)
        v_sm = project(h, seg(2), F32, d, row0=n_p, n_rows=n_s)
        f_b = project(h, seg(4), F32, d)
        zr = project(h, jnp.concatenate([seg(3), seg(5), seg(6), seg(7), seg(8)], axis=1), BF16, d)

        post = 1.0 - lam_init
        oa_p = attention_prompt(q_a, k_pr, v_pr, lam, slopes, subln_g[l], batch, lp, heads, 0, post)
        oa_s = attention_sample(q_a, k_sm, v_sm, cache_k.reshape(depth * dec_batch, -1, 2 * A_HD),
                                cache_v.reshape(depth * dec_batch, -1, 2 * A_HD), l * dec_batch, lam, slopes,
                                subln_g[l], dec_batch, dec_seq, heads, n_p // dec_seq, post)

        ob_p, sp = hgrn2(zr, f_b, lower_bounds[l], hgrn_norm_g[l], None, batch, lp, b_heads, 0, 4)
        ob_s, ss = hgrn2(zr, f_b, lower_bounds[l], hgrn_norm_g[l], state_hgrn[l], dec_batch, dec_seq,
                         b_heads, n_p // dec_seq, 4)

        x1, h2, logits = merge_branches(oa_p, oa_s, ob_p, ob_s, zr, x, w_a[l].astype(BF16),
                                        w_b[l].astype(BF16), w_o[l].astype(BF16),
                                        norm_ffn_g[l].astype(F32), w_router[l].astype(F32),
                                        b_router[l].astype(F32))

        gates, slots, items = _route(logits, n_experts, tm_moe, tm_tok)
        w1g, w1u = deinterleave_gate_up(w_mlp1[l], _pick_block(d, (1024, 512, 256, 128)))
        x_sorted = moe_dispatch(h2, slots, tm_tok)
        rows = moe_ffn(x_sorted, items, w1g, w1u,
                       b_mlp1[l][:, None, 0::2].astype(F32), b_mlp1[l][:, None, 1::2].astype(F32),
                       w_mlp2[l].astype(BF16), b_mlp2[l][:, None, :].astype(F32), tm_moe)
        x = moe_combine(rows, slots, gates, x1, tm_comb)

        hd = 2 * A_HD
        k_p.append(k_pr.reshape(batch, lp, heads, hd))
        v_p.append(v_pr.reshape(batch, lp, heads, hd))
        k_s.append(k_sm.reshape(dec_batch, dec_seq, heads, hd))
        v_s.append(v_sm.reshape(dec_batch, dec_seq, heads, hd))
        s_p.append(sp)
        s_s.append(ss)

    y_prompt = x[:n_p].reshape(batch, lp, d)[:, N_META:]
    y_sample = x[n_p:].reshape(dec_batch, dec_seq, d)
    return (y_prompt, y_sample, jnp.stack(k_p), jnp.stack(v_p), jnp.stack(s_p),
            jnp.stack(k_s), jnp.stack(v_s), jnp.stack(s_s))
```
